```python
import math
import jax, jax.numpy as jnp
from jax import lax
import numpy as np

D_MODEL = 1024
BATCH = 8
SEQ = 8192
DEPTH = 4

ROPE_THETA = 500000.0
NORM_EPS = 1e-6
QBLOCK = 128
N_BRANCH = 3

MLA_HEADS = 8
MLA_NOPE = 64
MLA_ROPE = 32
MLA_V = 64
MLA_Q_RANK = 384
MLA_KV_RANK = 256

HEAD_DIM = 64
PARTIAL_ROT_DIM = HEAD_DIM // 4

DIL_CONFIGS = ((128, 1), (512, 4), (2048, 16))
DIL_GROUPS = len(DIL_CONFIGS)
DIL_HEADS = 6

DIFF_HEADS = 4

D_FF = 2816
CONV_WIDTH = 3

GATE_COLS = N_BRANCH * D_MODEL
MLA_COLS = MLA_Q_RANK + MLA_KV_RANK + MLA_ROPE
DIL_COLS = DIL_GROUPS * 3 * DIL_HEADS * HEAD_DIM
DIFF_COLS = 3 * DIFF_HEADS * 2 * HEAD_DIM
IN_COLS = GATE_COLS + MLA_COLS + DIL_COLS + DIFF_COLS
SPLITS = [GATE_COLS, GATE_COLS + MLA_COLS, GATE_COLS + MLA_COLS + DIL_COLS]

MLA_OUT = MLA_HEADS * MLA_V
DIL_OUT = DIL_HEADS * HEAD_DIM
DIFF_OUT = DIFF_HEADS * 2 * HEAD_DIM

kernel_name = 'hybrid_mla_dilated_diff_encoder'


def rms_norm(x, g):
    x32 = x.astype(jnp.float32)
    y = x32 * lax.rsqrt(jnp.mean(x32 * x32, axis=-1, keepdims=True) + NORM_EPS)
    return (y * g.astype(jnp.float32)).astype(x.dtype)


def rope_tables(positions, rot_dim):
    inv = ROPE_THETA ** (-jnp.arange(0, rot_dim, 2, dtype=jnp.float32) / rot_dim)
    ang = positions.astype(jnp.float32)[..., None] * inv
    return jnp.cos(ang), jnp.sin(ang)


def apply_rope(x, cos, sin):
    r = 2 * cos.shape[-1]
    shape = cos.shape[:2] + (1,) * (x.ndim - 3) + cos.shape[-1:]
    c = cos.reshape(shape)
    s = sin.reshape(shape)
    xr = x[..., :r].astype(jnp.float32)
    x1, x2 = xr[..., : r // 2], xr[..., r // 2:]
    rot = jnp.concatenate([x1 * c - x2 * s, x2 * c + x1 * s], axis=-1).astype(x.dtype)
    return jnp.concatenate([rot, x[..., r:]], axis=-1)


def blocked_softmax_attention(q, k, v, scale):
    B, S, H, dk = q.shape
    nq = S // QBLOCK
    qb = q.reshape(B, nq, QBLOCK, H, dk).transpose(1, 0, 2, 3, 4)

    def one(qblk):
        s = jnp.einsum('bqhd,bkhd->bhqk', qblk, k, preferred_element_type=jnp.float32) * scale
        p = jax.nn.softmax(s, axis=-1).astype(v.dtype)
        return jnp.einsum('bhqk,bkhd->bqhd', p, v)

    out = lax.map(one, qb)
    return out.transpose(1, 0, 2, 3, 4).reshape(B, S, H, v.shape[-1])


def blocked_diff_attention(q, k, v, lam, scale):
    B, S, H, _, dh = q.shape
    nq = S // QBLOCK
    qb = q.reshape(B, nq, QBLOCK, H, 2, dh).transpose(1, 0, 2, 3, 4, 5)

    def one(qblk):
        s = jnp.einsum('bqhmd,bkhmd->bhmqk', qblk, k, preferred_element_type=jnp.float32) * scale
        p = jax.nn.softmax(s, axis=-1)
        a = (p[:, :, 0] - lam * p[:, :, 1]).astype(v.dtype)
        return jnp.einsum('bhqk,bkhd->bqhd', a, v)

    out = lax.map(one, qb)
    return out.transpose(1, 0, 2, 3, 4).reshape(B, S, H, v.shape[-1])


def banded_attention(q, k, v, radius, scale):
    lead = q.shape[:-2]
    L, dh = q.shape[-2], q.shape[-1]
    blk = radius
    nb = -(-L // blk)
    lp = nb * blk
    nlead = len(lead)
    qb = jnp.pad(q, [(0, 0)] * nlead + [(0, lp - L), (0, 0)]).reshape(lead + (nb, blk, dh))

    def windows(t):
        tb = jnp.pad(t, [(0, 0)] * nlead + [(blk, lp - L + blk), (0, 0)])
        tb = tb.reshape(lead + (nb + 2, blk, t.shape[-1]))
        return jnp.concatenate([tb[..., :-2, :, :], tb[..., 1:-1, :, :], tb[..., 2:, :, :]], axis=-2)

    kw, vw = windows(k), windows(v)
    qpos = jnp.arange(lp).reshape(nb, blk, 1)
    kpos = (jnp.arange(nb)[:, None] * blk + jnp.arange(3 * blk)[None, :] - blk)[:, None, :]
    valid = (jnp.abs(qpos - kpos) <= radius) & (kpos >= 0) & (kpos < L)
    s = jnp.einsum('...nqd,...nkd->...nqk', qb, kw, preferred_element_type=jnp.float32) * scale
    s = jnp.where(valid, s, -jnp.inf)
    lse = jax.nn.logsumexp(s, axis=-1, keepdims=True)
    p = jnp.exp(s - lse).astype(v.dtype)
    o = jnp.einsum('...nqk,...nkd->...nqd', p, vw)
    o = o.reshape(lead + (lp, v.shape[-1]))[..., :L, :]
    lse = lse.reshape(lead + (lp,))[..., :L]
    return o, lse


def mla_mixer(mla_in, q_norm, w_uq, kv_norm, w_ukv, cos_m, sin_m):
    B, S, _ = mla_in.shape
    c_q, c_kv, k_rope = jnp.split(mla_in, [MLA_Q_RANK, MLA_Q_RANK + MLA_KV_RANK], axis=-1)
    q = (rms_norm(c_q, q_norm) @ w_uq).reshape(B, S, MLA_HEADS, MLA_NOPE + MLA_ROPE)
    q = jnp.concatenate([q[..., :MLA_NOPE], apply_rope(q[..., MLA_NOPE:], cos_m, sin_m)], axis=-1)
    kv = (rms_norm(c_kv, kv_norm) @ w_ukv).reshape(B, S, MLA_HEADS, MLA_NOPE + MLA_V)
    k_nope, v = kv[..., :MLA_NOPE], kv[..., MLA_NOPE:]
    k_rope = apply_rope(k_rope, cos_m, sin_m)
    k = jnp.concatenate([k_nope, jnp.broadcast_to(k_rope[:, :, None, :], (B, S, MLA_HEADS, MLA_ROPE))], axis=-1)
    o = blocked_softmax_attention(q, k, v, (MLA_NOPE + MLA_ROPE) ** -0.5)
    return o.reshape(B, S, MLA_OUT)


def dilated_mixer(dil_in, cos_p, sin_p):
    B, S, _ = dil_in.shape
    qkv = dil_in.reshape(B, S, DIL_GROUPS, 3, DIL_HEADS, HEAD_DIM)
    outs, lses = [], []
    for g, (window, dil) in enumerate(DIL_CONFIGS):
        q = apply_rope(qkv[:, :, g, 0], cos_p, sin_p)
        k = apply_rope(qkv[:, :, g, 1], cos_p, sin_p)
        v = qkv[:, :, g, 2]
        radius = window // (2 * dil)

        def to_sub(t):
            return t.reshape(B, S // dil, dil, DIL_HEADS, HEAD_DIM).transpose(0, 2, 3, 1, 4)

        o, lse = banded_attention(to_sub(q), to_sub(k), to_sub(v), radius, HEAD_DIM ** -0.5)
        outs.append(o.transpose(0, 3, 1, 2, 4).reshape(B, S, DIL_HEADS, HEAD_DIM))
        lses.append(lse.transpose(0, 3, 1, 2).reshape(B, S, DIL_HEADS))
    alpha = jax.nn.softmax(jnp.stack(lses, axis=0), axis=0)
    o = jnp.sum(alpha[..., None] * jnp.stack(outs, axis=0).astype(jnp.float32), axis=0)
    return o.astype(dil_in.dtype).reshape(B, S, DIL_OUT)


def diff_mixer(diff_in, lam_vecs, subln, lam_init, cos_p, sin_p):
    B, S, _ = diff_in.shape
    dq, dk, dv = jnp.split(diff_in, 3, axis=-1)
    dq = apply_rope(dq.reshape(B, S, DIFF_HEADS, 2, HEAD_DIM), cos_p, sin_p)
    dk = apply_rope(dk.reshape(B, S, DIFF_HEADS, 2, HEAD_DIM), cos_p, sin_p)
    dv = dv.reshape(B, S, DIFF_HEADS, 2 * HEAD_DIM)
    lv = lam_vecs.astype(jnp.float32)
    lam = jnp.exp(jnp.sum(lv[0] * lv[1])) - jnp.exp(jnp.sum(lv[2] * lv[3])) + lam_init
    o = blocked_diff_attention(dq, dk, dv, lam, HEAD_DIM ** -0.5)
    o = rms_norm(o, subln) * (1.0 - lam_init)
    return o.reshape(B, S, DIFF_OUT)


def conv_gated_mlp(h, w_up, conv_w, w_down):
    S = h.shape[1]
    u = h @ w_up
    pad = CONV_WIDTH // 2
    up = jnp.pad(u, ((0, 0), (pad, pad), (0, 0)))
    u = sum(up[:, t:t + S] * conv_w[t] for t in range(CONV_WIDTH))
    gate, val = jnp.split(u, 2, axis=-1)
    return (jax.nn.gelu(gate, approximate=True) * val) @ w_down


def lambda_init(layer):
    return 0.8 - 0.6 * math.exp(-0.3 * layer)


def setup_inputs(seed: int = 0) -> dict:
    key = jax.random.key(seed)
    ks = jax.random.split(key, 24)
    f32 = jnp.float32

    def nrm(k, shape, scale):
        return jax.random.normal(k, shape, f32) * scale

    def gain(k, n):
        return 1.0 + 0.02 * jax.random.normal(k, (DEPTH, n), f32)

    x = jax.random.normal(ks[0], (BATCH, SEQ, D_MODEL), f32)
    positions = jnp.arange(SEQ, dtype=jnp.int32)[None, :] + jax.random.randint(ks[1], (BATCH, 1), 0, 1024, dtype=jnp.int32)
    return {
        'x': x,
        'positions': positions,
        'attn_pre_norm': gain(ks[2], D_MODEL),
        'w_in': nrm(ks[3], (DEPTH, D_MODEL, IN_COLS), D_MODEL ** -0.5),
        'mla_q_norm': gain(ks[4], MLA_Q_RANK),
        'mla_w_uq': nrm(ks[5], (DEPTH, MLA_Q_RANK, MLA_HEADS * (MLA_NOPE + MLA_ROPE)), MLA_Q_RANK ** -0.5),
        'mla_kv_norm': gain(ks[6], MLA_KV_RANK),
        'mla_w_ukv': nrm(ks[7], (DEPTH, MLA_KV_RANK, MLA_HEADS * (MLA_NOPE + MLA_V)), MLA_KV_RANK ** -0.5),
        'diff_lambda': nrm(ks[8], (DEPTH, 4, HEAD_DIM), 0.1),
        'diff_subln': gain(ks[9], 2 * HEAD_DIM),
        'w_branch_mla': nrm(ks[10], (DEPTH, MLA_OUT, D_MODEL), MLA_OUT ** -0.5),
        'w_branch_dil': nrm(ks[11], (DEPTH, DIL_OUT, D_MODEL), DIL_OUT ** -0.5),
        'w_branch_diff': nrm(ks[12], (DEPTH, DIFF_OUT, D_MODEL), DIFF_OUT ** -0.5),
        'w_out': nrm(ks[13], (DEPTH, D_MODEL, D_MODEL), D_MODEL ** -0.5),
        'attn_post_norm': gain(ks[14], D_MODEL),
        'ffn_pre_norm': gain(ks[15], D_MODEL),
        'w_up': nrm(ks[16], (DEPTH, D_MODEL, 2 * D_FF), D_MODEL ** -0.5),
        'ffn_conv': nrm(ks[17], (DEPTH, CONV_WIDTH, 2 * D_FF), CONV_WIDTH ** -0.5),
        'w_down': nrm(ks[18], (DEPTH, D_FF, D_MODEL), D_FF ** -0.5),
        'ffn_post_norm': gain(ks[19], D_MODEL),
    }


def reference(x, positions, attn_pre_norm, w_in, mla_q_norm, mla_w_uq, mla_kv_norm, mla_w_ukv, diff_lambda, diff_subln, w_branch_mla, w_branch_dil, w_branch_diff, w_out, attn_post_norm, ffn_pre_norm, w_up, ffn_conv, w_down, ffn_post_norm):
    cos_p, sin_p = rope_tables(positions, PARTIAL_ROT_DIM)
    cos_m, sin_m = rope_tables(positions, MLA_ROPE)
    for l in range(DEPTH):
        h = rms_norm(x, attn_pre_norm[l])
        proj = jnp.einsum('bsd,dc->bsc', h, w_in[l])
        gates, mla_in, dil_in, diff_in = jnp.split(proj, SPLITS, axis=-1)
        o_mla = mla_mixer(mla_in, mla_q_norm[l], mla_w_uq[l], mla_kv_norm[l], mla_w_ukv[l], cos_m, sin_m)
        o_dil = dilated_mixer(dil_in, cos_p, sin_p)
        o_diff = diff_mixer(diff_in, diff_lambda[l], diff_subln[l], lambda_init(l), cos_p, sin_p)
        g_mla, g_dil, g_diff = jnp.split(jax.nn.sigmoid(gates), N_BRANCH, axis=-1)
        merged = g_mla * (o_mla @ w_branch_mla[l]) + g_dil * (o_dil @ w_branch_dil[l]) + g_diff * (o_diff @ w_branch_diff[l])
        x = x + rms_norm(merged @ w_out[l], attn_post_norm[l])
        h = rms_norm(x, ffn_pre_norm[l])
        x = x + rms_norm(conv_gated_mlp(h, w_up[l], ffn_conv[l], w_down[l]), ffn_post_norm[l])
    return x
```

```python
import functools
import math

import jax
import jax.numpy as jnp
from jax import lax
from jax.experimental import pallas as pl
from jax.experimental.pallas import tpu as pltpu

F32 = jnp.float32
BF16 = jnp.bfloat16

D_MODEL = 1024
ROPE_THETA = 500000.0
NORM_EPS = 1e-6

MLA_HEADS = 8
MLA_NOPE = 64
MLA_ROPE = 32
MLA_V = 64
MLA_Q_RANK = 384
MLA_KV_RANK = 256

HEAD_DIM = 64
PARTIAL_ROT_DIM = HEAD_DIM // 4
DIL_CONFIGS = ((128, 1), (512, 4), (2048, 16))
DIL_GROUPS = len(DIL_CONFIGS)
DIL_HEADS = 6
DIFF_HEADS = 4
D_FF = 2816
CONV_WIDTH = 3

GATE_COLS = 3 * D_MODEL
MLA_COLS = MLA_Q_RANK + MLA_KV_RANK + MLA_ROPE
DIL_COLS = DIL_GROUPS * 3 * DIL_HEADS * HEAD_DIM
DIFF_COLS = 3 * DIFF_HEADS * 2 * HEAD_DIM
DIL_OUT = DIL_HEADS * HEAD_DIM
DIFF_OUT = DIFF_HEADS * 2 * HEAD_DIM

LANES = 128
LOG2E = 1.4426950408889634
NEG_BIG = -1e30
VMEM_LIMIT = 56 * 1024 * 1024

TM_PROJ = 512
TM_MERGE = 256
TM_FFN = 512
TF_FFN = 1408
FFN_HALO = 16
TQ_FLASH = 256
TK_FLASH = 1024
TJ_DIL = 512
DIL_RADIUS = 64
DIL_QSUB = 128
DIL_KWIN = DIL_QSUB + 2 * DIL_RADIUS


def _cparams(sem):
    return pltpu.CompilerParams(dimension_semantics=sem, vmem_limit_bytes=VMEM_LIMIT)


def _rms(x, g):
    ms = jnp.mean(x * x, axis=-1, keepdims=True)
    return x * lax.rsqrt(ms + NORM_EPS) * g


def _rope128(z, c, sa, sb, shift):
    return z * c + pltpu.roll(z, LANES - shift, 1) * sa + pltpu.roll(z, shift, 1) * sb


def _inproj_kernel(x_ref, g_ref, w_ref, *rest, rope_chunks, n_chunks):
    if rope_chunks:
        c_ref, sa_ref, sb_ref, o_ref = rest
    else:
        (o_ref,) = rest
    h = _rms(x_ref[...], g_ref[...]).astype(BF16)
    y = jnp.dot(h, w_ref[...], preferred_element_type=F32)
    for c in range(n_chunks):
        yc = y[:, c * LANES:(c + 1) * LANES]
        if c < rope_chunks:
            yc = _rope128(yc, c_ref[...], sa_ref[...], sb_ref[...], PARTIAL_ROT_DIM // 2)
        o_ref[:, c * LANES:(c + 1) * LANES] = yc.astype(o_ref.dtype)


def _inproj(x2, g, w, tables, *, tn, rope_chunks):
    t, d = x2.shape
    n = w.shape[1]
    tm = TM_PROJ
    in_specs = [
        pl.BlockSpec((tm, d), lambda j, i: (i, 0)),
        pl.BlockSpec((1, d), lambda j, i: (0, 0)),
        pl.BlockSpec((d, tn), lambda j, i: (0, j)),
    ]
    args = [x2, g, w]
    if rope_chunks:
        in_specs += [pl.BlockSpec((tm, LANES), lambda j, i: (i, 0))] * 3
        args += list(tables)
    return pl.pallas_call(
        functools.partial(_inproj_kernel, rope_chunks=rope_chunks, n_chunks=tn // LANES),
        grid=(n // tn, t // tm),
        in_specs=in_specs,
        out_specs=pl.BlockSpec((tm, tn), lambda j, i: (i, j)),
        out_shape=jax.ShapeDtypeStruct((t, n), BF16),
        name="inproj_rope" if rope_chunks else "inproj_gates",
        compiler_params=_cparams(("parallel", "parallel")),
    )(*args)


def _mla_prep_kernel(x_ref, g_ref, win_ref, gq_ref, gkv_ref, wuq_ref, wukv_ref,
                     c_ref, sa_ref, sb_ref, q_ref, k_ref, v_ref):
    h = _rms(x_ref[...], g_ref[...]).astype(BF16)
    y = jnp.dot(h, win_ref[...], preferred_element_type=F32)
    cq = y[:, :MLA_Q_RANK]
    ckv = y[:, MLA_Q_RANK:MLA_Q_RANK + MLA_KV_RANK]
    kr = y[:, MLA_Q_RANK + MLA_KV_RANK:]
    qn = _rms(cq, gq_ref[...]).astype(BF16)
    kvn = _rms(ckv, gkv_ref[...]).astype(BF16)
    q = jnp.dot(qn, wuq_ref[...], preferred_element_type=F32)
    kv = jnp.dot(kvn, wukv_ref[...], preferred_element_type=F32)
    c, sa, sb = c_ref[...], sa_ref[...], sb_ref[...]
    krr = _rope128(kr, c, sa, sb, MLA_ROPE // 2)
    lane = lax.broadcasted_iota(jnp.int32, krr.shape, 1)
    for hd in range(MLA_HEADS):
        sl = slice(hd * LANES, (hd + 1) * LANES)
        q_ref[:, sl] = _rope128(q[:, sl], c, sa, sb, MLA_ROPE // 2).astype(BF16)
        kvc = kv[:, sl]
        k_ref[:, sl] = jnp.where(lane < MLA_NOPE, kvc, krr).astype(BF16)
        v_ref[:, sl] = kvc.astype(BF16)


def _mla_prep(x2, g, win, gq, gkv, wuq, wukv, tables):
    t, d = x2.shape
    tm = TM_PROJ
    hw = MLA_HEADS * LANES
    full = lambda a: pl.BlockSpec(a.shape, lambda i: (0,) * a.ndim)
    row = lambda w_: pl.BlockSpec((tm, w_), lambda i: (i, 0))
    out = jax.ShapeDtypeStruct((t, hw), BF16)
    return pl.pallas_call(
        _mla_prep_kernel,
        grid=(t // tm,),
        in_specs=[row(d), full(g), full(win), full(gq), full(gkv), full(wuq), full(wukv),
                  row(LANES), row(LANES), row(LANES)],
        out_specs=[row(hw)] * 3,
        out_shape=[out] * 3,
        name="mla_prep",
        compiler_params=_cparams(("parallel",)),
    )(x2, g, win, gq, gkv, wuq, wukv, *tables)


def _flash_init(m_ref, l_ref, acc_ref):
    m_ref[...] = jnp.full(m_ref.shape, -jnp.inf, F32)
    l_ref[...] = jnp.zeros(l_ref.shape, F32)
    acc_ref[...] = jnp.zeros(acc_ref.shape, F32)


def _flash_step(q, k, v, m_ref, l_ref, acc_ref):
    tk = k.shape[0]
    s = lax.dot_general(q, k, (((1,), (1,)), ((), ())), preferred_element_type=F32)
    m_prev = m_ref[...]
    m_new = jnp.maximum(m_prev, jnp.max(s, axis=1, keepdims=True))
    alpha = jnp.exp2(m_prev - m_new)
    p = jnp.exp2(s - jnp.tile(m_new, (1, tk // LANES)))
    psum = p[:, :LANES]
    for i in range(1, tk // LANES):
        psum = psum + p[:, i * LANES:(i + 1) * LANES]
    l_ref[...] = alpha * l_ref[...] + psum
    acc_ref[...] = alpha * acc_ref[...] + jnp.dot(p.astype(BF16), v, preferred_element_type=F32)
    m_ref[...] = m_new


def _flash_result(l_ref, acc_ref):
    return acc_ref[...] / jnp.sum(l_ref[...], axis=1, keepdims=True)


def _flash_mla_kernel(q_ref, k_ref, v_ref, o_ref, m_ref, l_ref, acc_ref, *, tk):
    _flash_init(m_ref, l_ref, acc_ref)
    q = q_ref[0]

    def body(c, carry):
        off = pl.multiple_of(c * tk, tk)
        _flash_step(q, k_ref[0, pl.ds(off, tk), :], v_ref[0, pl.ds(off, tk), :], m_ref, l_ref, acc_ref)
        return carry

    lax.fori_loop(0, k_ref.shape[1] // tk, body, 0)
    o_ref[0] = _flash_result(l_ref, acc_ref).astype(o_ref.dtype)


def _flash_mla(q, k, v):
    b, s, hw = q.shape
    nh = hw // LANES
    tq, tk = min(TQ_FLASH, s), min(TK_FLASH, s)
    qspec = pl.BlockSpec((1, tq, LANES), lambda bi, h, i: (bi, i, h))
    kspec = pl.BlockSpec((1, s, LANES), lambda bi, h, i: (bi, 0, h))
    return pl.pallas_call(
        functools.partial(_flash_mla_kernel, tk=tk),
        grid=(b, nh, s // tq),
        in_specs=[qspec, kspec, kspec],
        out_specs=qspec,
        out_shape=jax.ShapeDtypeStruct((b, s, hw), BF16),
        scratch_shapes=[pltpu.VMEM((tq, LANES), F32)] * 3,
        name="flash_mla",
        compiler_params=_cparams(("parallel", "parallel", "arbitrary")),
    )(q, k, v)


def _flash_diff_kernel(q_ref, k_ref, v_ref, lam_ref, subln_ref, o_ref,
                       m1, l1, a1, m2, l2, a2, *, tk, lam_init):
    _flash_init(m1, l1, a1)
    _flash_init(m2, l2, a2)
    q = q_ref[0]
    lane = lax.broadcasted_iota(jnp.int32, q.shape, 1)
    zero = jnp.zeros_like(q)
    qa = jnp.where(lane < HEAD_DIM, q, zero)
    qb = jnp.where(lane >= HEAD_DIM, q, zero)

    def body(c, carry):
        off = pl.multiple_of(c * tk, tk)
        k = k_ref[0, pl.ds(off, tk), :]
        v = v_ref[0, pl.ds(off, tk), :]
        _flash_step(qa, k, v, m1, l1, a1)
        _flash_step(qb, k, v, m2, l2, a2)
        return carry

    lax.fori_loop(0, k_ref.shape[1] // tk, body, 0)
    lv = lam_ref[...]
    lam = (jnp.exp(jnp.sum(lv[0:1] * lv[1:2], axis=1, keepdims=True))
           - jnp.exp(jnp.sum(lv[2:3] * lv[3:4], axis=1, keepdims=True)) + lam_init)
    o = _flash_result(l1, a1) - lam * _flash_result(l2, a2)
    o_ref[0] = (_rms(o, subln_ref[...]) * (1.0 - lam_init)).astype(o_ref.dtype)


def _flash_diff(proj, lam_vecs, subln, lam_init):
    b, s, _ = proj.shape
    nh = DIFF_HEADS
    tq, tk = min(TQ_FLASH, s), min(TK_FLASH, s)
    return pl.pallas_call(
        functools.partial(_flash_diff_kernel, tk=tk, lam_init=lam_init),
        grid=(b, nh, s // tq),
        in_specs=[
            pl.BlockSpec((1, tq, LANES), lambda bi, h, i: (bi, i, h)),
            pl.BlockSpec((1, s, LANES), lambda bi, h, i: (bi, 0, nh + h)),
            pl.BlockSpec((1, s, LANES), lambda bi, h, i: (bi, 0, 2 * nh + h)),
            pl.BlockSpec(lam_vecs.shape, lambda bi, h, i: (0, 0)),
            pl.BlockSpec(subln.shape, lambda bi, h, i: (0, 0)),
        ],
        out_specs=pl.BlockSpec((1, tq, LANES), lambda bi, h, i: (bi, i, h)),
        out_shape=jax.ShapeDtypeStruct((b, s, DIFF_OUT), BF16),
        scratch_shapes=[pltpu.VMEM((tq, LANES), F32)] * 6,
        name="flash_diff",
        compiler_params=_cparams(("parallel", "parallel", "arbitrary")),
    )(proj, proj, proj, lam_vecs, subln)


def _dil_kernel(q_ref, kp_ref, kc_ref, kn_ref, vp_ref, vc_ref, vn_ref, o_ref, lse_ref, *, tj, lsub):
    j0 = pl.program_id(2) * tj
    r = DIL_RADIUS
    row = lax.broadcasted_iota(jnp.int32, (DIL_QSUB, DIL_KWIN), 0)
    col = lax.broadcasted_iota(jnp.int32, (DIL_QSUB, DIL_KWIN), 1)
    band = jnp.abs(col - r - row) <= r
    lane = lax.broadcasted_iota(jnp.int32, (DIL_QSUB, LANES), 1)
    low = lane < HEAD_DIM
    for c in range(DIL_OUT // LANES):
        sl = slice(c * LANES, (c + 1) * LANES)
        kcat = jnp.concatenate([kp_ref[0, tj - r:tj, sl], kc_ref[0, :, sl], kn_ref[0, 0:r, sl]], axis=0)
        vcat = jnp.concatenate([vp_ref[0, tj - r:tj, sl], vc_ref[0, :, sl], vn_ref[0, 0:r, sl]], axis=0)
        for i in range(tj // DIL_QSUB):
            kpos = j0 + (i * DIL_QSUB - r) + col
            valid = band & (kpos >= 0) & (kpos < lsub)
            q = q_ref[0, i * DIL_QSUB:(i + 1) * DIL_QSUB, sl]
            kw = kcat[i * DIL_QSUB:i * DIL_QSUB + DIL_KWIN]
            vw = vcat[i * DIL_QSUB:i * DIL_QSUB + DIL_KWIN]
            outs, lses = [], []
            for hh in range(2):
                qh = jnp.where(low if hh == 0 else ~low, q, jnp.zeros_like(q))
                s = lax.dot_general(qh, kw, (((1,), (1,)), ((), ())), preferred_element_type=F32)
                s = jnp.where(valid, s, NEG_BIG)
                m = jnp.max(s, axis=1, keepdims=True)
                p = jnp.exp2(s - m)
                l = jnp.sum(p, axis=1, keepdims=True)
                pv = jnp.dot(p.astype(BF16), vw, preferred_element_type=F32)
                outs.append(pv / l)
                lses.append(jnp.broadcast_to(m + jnp.log2(l), (DIL_QSUB, LANES)))
            rows = slice(i * DIL_QSUB, (i + 1) * DIL_QSUB)
            o_ref[0, rows, sl] = jnp.where(low, outs[0], outs[1])
            lse_ref[0, rows, sl] = jnp.where(low, lses[0], lses[1])


def _dilated(proj, g, dil):
    b, s, ncol = proj.shape
    lsub = s // dil
    tj = min(TJ_DIL, lsub)
    nblk = lsub // tj
    w = DIL_OUT
    per_tok = ncol // w
    view = proj.reshape(b, lsub, dil * ncol)

    def spec(t, shift):
        def imap(bi, ri, ji):
            jj = jnp.clip(ji + shift, 0, nblk - 1)
            return (bi, jj, ri * per_tok + g * 3 + t)
        return pl.BlockSpec((1, tj, w), imap)

    ospec = pl.BlockSpec((1, tj, w), lambda bi, ri, ji: (bi, ji, ri))
    oshape = jax.ShapeDtypeStruct((b, lsub, dil * w), F32)
    o, lse = pl.pallas_call(
        functools.partial(_dil_kernel, tj=tj, lsub=lsub),
        grid=(b, dil, nblk),
        in_specs=[spec(0, 0), spec(1, -1), spec(1, 0), spec(1, 1), spec(2, -1), spec(2, 0), spec(2, 1)],
        out_specs=[ospec, ospec],
        out_shape=[oshape, oshape],
        name=f"dilated_d{dil}",
        compiler_params=_cparams(("parallel", "parallel", "parallel")),
    )(view, view, view, view, view, view, view)
    return o.reshape(b * s, w), lse.reshape(b * s, w)


def _merge_kernel(x_ref, gate_ref, omla_ref, od0, od1, od2, ls0, ls1, ls2, odiff_ref,
                  wbm_ref, wbd_ref, wbf_ref, wo_ref, pg_ref, o_ref):
    l0, l1, l2 = ls0[...], ls1[...], ls2[...]
    mx = jnp.maximum(jnp.maximum(l0, l1), l2)
    e0, e1, e2 = jnp.exp2(l0 - mx), jnp.exp2(l1 - mx), jnp.exp2(l2 - mx)
    odil = ((e0 * od0[...] + e1 * od1[...] + e2 * od2[...]) / (e0 + e1 + e2)).astype(BF16)
    d = D_MODEL
    y = jax.nn.sigmoid(gate_ref[:, 0:d].astype(F32)) * jnp.dot(omla_ref[...], wbm_ref[...], preferred_element_type=F32)
    y = y + jax.nn.sigmoid(gate_ref[:, d:2 * d].astype(F32)) * jnp.dot(odil, wbd_ref[...], preferred_element_type=F32)
    y = y + jax.nn.sigmoid(gate_ref[:, 2 * d:3 * d].astype(F32)) * jnp.dot(odiff_ref[...], wbf_ref[...], preferred_element_type=F32)
    z = jnp.dot(y.astype(BF16), wo_ref[...], preferred_element_type=F32)
    o_ref[...] = x_ref[...] + _rms(z, pg_ref[...])


def _merge(x2, gates, omla, odil, lses, odiff, wbm, wbd, wbf, wo, pg):
    t, d = x2.shape
    tm = TM_MERGE
    full = lambda a: pl.BlockSpec(a.shape, lambda i: (0,) * a.ndim)
    row = lambda a: pl.BlockSpec((tm, a.shape[1]), lambda i: (i, 0))
    acts = [x2, gates, omla, *odil, *lses, odiff]
    wts = [wbm, wbd, wbf, wo, pg]
    return pl.pallas_call(
        _merge_kernel,
        grid=(t // tm,),
        in_specs=[row(a) for a in acts] + [full(a) for a in wts],
        out_specs=pl.BlockSpec((tm, d), lambda i: (i, 0)),
        out_shape=jax.ShapeDtypeStruct((t, d), F32),
        name="merge",
        compiler_params=_cparams(("parallel",)),
    )(*acts, *wts)


def _ffn_kernel(xp_ref, x_ref, xn_ref, g_ref, wg_ref, wv_ref, cg_ref, cv_ref, wd_ref, pg_ref,
                o_ref, h_ref, acc_ref, *, tm, seq):
    i = pl.program_id(0)
    c = pl.program_id(1)
    hl = FFN_HALO

    @pl.when(c == 0)
    def _():
        g = g_ref[...]
        keep_p = ((i * tm) % seq != 0).astype(F32)
        keep_n = (((i + 1) * tm) % seq != 0).astype(F32)
        h_ref[0:hl, :] = (_rms(xp_ref[...], g) * keep_p).astype(BF16)
        h_ref[hl:hl + tm, :] = _rms(x_ref[...], g).astype(BF16)
        h_ref[hl + tm:2 * hl + tm, :] = (_rms(xn_ref[...], g) * keep_n).astype(BF16)
        acc_ref[...] = jnp.zeros(acc_ref.shape, F32)

    h = h_ref[...]
    n_ext = tm + 2 * hl

    def conv(u, cw):
        up = pltpu.roll(u, 1, 0)
        dn = pltpu.roll(u, n_ext - 1, 0)
        z = cw[0:1] * up + cw[1:2] * u + cw[2:3] * dn
        return z[hl:hl + tm]

    ug = conv(jnp.dot(h, wg_ref[...], preferred_element_type=F32), cg_ref[...])
    uv = conv(jnp.dot(h, wv_ref[...], preferred_element_type=F32), cv_ref[...])
    a = (jax.nn.gelu(ug, approximate=True) * uv).astype(BF16)
    acc_ref[...] += jnp.dot(a, wd_ref[...], preferred_element_type=F32)

    @pl.when(c == pl.num_programs(1) - 1)
    def _():
        o_ref[...] = x_ref[...] + _rms(acc_ref[...], pg_ref[...])


def _ffn(x2, g, wup, convw, wdown, pg, seq):
    t, d = x2.shape
    tm = min(TM_FFN, seq)
    tf = TF_FFN
    nch = D_FF // tf
    hl = FFN_HALO
    rb = tm // hl
    nhb = t // hl
    return pl.pallas_call(
        functools.partial(_ffn_kernel, tm=tm, seq=seq),
        grid=(t // tm, nch),
        in_specs=[
            pl.BlockSpec((hl, d), lambda i, c: (jnp.maximum(i * rb - 1, 0), 0)),
            pl.BlockSpec((tm, d), lambda i, c: (i, 0)),
            pl.BlockSpec((hl, d), lambda i, c: (jnp.minimum((i + 1) * rb, nhb - 1), 0)),
            pl.BlockSpec((1, d), lambda i, c: (0, 0)),
            pl.BlockSpec((d, tf), lambda i, c: (0, c)),
            pl.BlockSpec((d, tf), lambda i, c: (0, c + nch)),
            pl.BlockSpec((CONV_WIDTH, tf), lambda i, c: (0, c)),
            pl.BlockSpec((CONV_WIDTH, tf), lambda i, c: (0, c + nch)),
            pl.BlockSpec((tf, d), lambda i, c: (c, 0)),
            pl.BlockSpec((1, d), lambda i, c: (0, 0)),
        ],
        out_specs=pl.BlockSpec((tm, d), lambda i, c: (i, 0)),
        out_shape=jax.ShapeDtypeStruct((t, d), F32),
        scratch_shapes=[pltpu.VMEM((tm + 2 * hl, d), BF16), pltpu.VMEM((tm, d), F32)],
        name="conv_ffn",
        compiler_params=_cparams(("parallel", "arbitrary")),
    )(x2, x2, x2, g, wup, wup, convw, convw, wdown, pg)


def _rope_tables(positions, rot_dim, period, base):
    half = rot_dim // 2
    inv = ROPE_THETA ** (-jnp.arange(0, rot_dim, 2, dtype=F32) / rot_dim)
    ang = positions.astype(F32).reshape(-1, 1) * inv[None, :]
    cos, sin = jnp.cos(ang), jnp.sin(ang)
    lane = jnp.arange(LANES)
    rel = lane % period - base
    idx = jnp.clip(rel, 0, rot_dim - 1) % half
    first = (rel >= 0) & (rel < half)
    second = (rel >= half) & (rel < rot_dim)
    cg, sg = cos[:, idx], sin[:, idx]
    c = jnp.where((first | second)[None, :], cg, 1.0)
    sa = jnp.where(first[None, :], -sg, 0.0)
    sb = jnp.where(second[None, :], sg, 0.0)
    return c, sa, sb


def _layer_weights(l, w_in, mla_w_uq, mla_w_ukv, w_branch_mla):
    d = D_MODEL
    wi = w_in[l]
    o1 = GATE_COLS
    o2 = o1 + MLA_COLS
    o3 = o2 + DIL_COLS
    w_gate = wi[:, :o1].astype(BF16)
    wm = wi[:, o1:o2]
    z = lambda n: jnp.zeros((d, n), F32)
    w_mla_in = jnp.concatenate(
        [wm[:, :MLA_Q_RANK + MLA_KV_RANK], z(MLA_NOPE), wm[:, MLA_Q_RANK + MLA_KV_RANK:], z(LANES - MLA_NOPE - MLA_ROPE)],
        axis=1).astype(BF16)
    qscale = LOG2E * HEAD_DIM ** -0.5
    wd = wi[:, o2:o3].reshape(d, DIL_GROUPS, 3, DIL_OUT)
    wd = wd * jnp.array([qscale, 1.0, 1.0], F32)[None, None, :, None]
    w_dil = wd.reshape(d, DIL_COLS).astype(BF16)
    wf = wi[:, o3:].reshape(d, 3, DIFF_OUT)
    wf = wf * jnp.array([qscale, 1.0, 1.0], F32)[None, :, None]
    w_diff = wf.reshape(d, DIFF_COLS).astype(BF16)
    qk = MLA_NOPE + MLA_ROPE
    wq = mla_w_uq[l].reshape(MLA_Q_RANK, MLA_HEADS, qk) * (LOG2E * qk ** -0.5)
    wq = jnp.pad(wq, ((0, 0), (0, 0), (0, LANES - qk))).reshape(MLA_Q_RANK, MLA_HEADS * LANES).astype(BF16)
    wkv = mla_w_ukv[l].astype(BF16)
    wb = w_branch_mla[l].reshape(MLA_HEADS, MLA_V, d)
    wb = jnp.pad(wb, ((0, 0), (LANES - MLA_V, 0), (0, 0))).reshape(MLA_HEADS * LANES, d).astype(BF16)
    return w_gate, w_mla_in, w_dil, w_diff, wq, wkv, wb


def _lambda_init(layer):
    return 0.8 - 0.6 * math.exp(-0.3 * layer)


def kernel(x, positions, attn_pre_norm, w_in, mla_q_norm, mla_w_uq, mla_kv_norm, mla_w_ukv, diff_lambda,
           diff_subln, w_branch_mla, w_branch_dil, w_branch_diff, w_out, attn_post_norm, ffn_pre_norm,
           w_up, ffn_conv, w_down, ffn_post_norm):
    b, s, d = x.shape
    t = b * s
    depth = w_in.shape[0]
    tab_p = _rope_tables(positions, PARTIAL_ROT_DIM, HEAD_DIM, 0)
    tab_m = _rope_tables(positions, MLA_ROPE, LANES, MLA_NOPE)
    x2 = x.reshape(t, d)
    row = lambda a: a.reshape(1, -1)
    for l in range(depth):
        w_gate, w_mla_in, w_dil, w_diff, wq, wkv, wbm = _layer_weights(l, w_in, mla_w_uq, mla_w_ukv, w_branch_mla)
        g_pre = row(attn_pre_norm[l])
        gates = _inproj(x2, g_pre, w_gate, None, tn=GATE_COLS // 2, rope_chunks=0)
        p_dil = _inproj(x2, g_pre, w_dil, tab_p, tn=DIL_COLS // DIL_GROUPS, rope_chunks=2 * DIL_OUT // LANES)
        p_diff = _inproj(x2, g_pre, w_diff, tab_p, tn=DIFF_COLS, rope_chunks=2 * DIFF_OUT // LANES)
        q_m, k_m, v_m = _mla_prep(x2, g_pre, w_mla_in, row(mla_q_norm[l]), row(mla_kv_norm[l]), wq, wkv, tab_m)
        hw = MLA_HEADS * LANES
        o_mla = _flash_mla(q_m.reshape(b, s, hw), k_m.reshape(b, s, hw), v_m.reshape(b, s, hw)).reshape(t, hw)
        o_diff = _flash_diff(p_diff.reshape(b, s, DIFF_COLS), diff_lambda[l], row(diff_subln[l]),
                             _lambda_init(l)).reshape(t, DIFF_OUT)
        p_dil3 = p_dil.reshape(b, s, DIL_COLS)
        dil = [_dilated(p_dil3, gi, dl) for gi, (_, dl) in enumerate(DIL_CONFIGS)]
        x2 = _merge(x2, gates, o_mla, [o for o, _ in dil], [ls for _, ls in dil], o_diff,
                    wbm, w_branch_dil[l].astype(BF16), w_branch_diff[l].astype(BF16), w_out[l].astype(BF16),
                    row(attn_post_norm[l]))
        x2 = _ffn(x2, row(ffn_pre_norm[l]), w_up[l].astype(BF16), ffn_conv[l], w_down[l].astype(BF16),
                  row(ffn_post_norm[l]), s)
    return x2.reshape(b, s, d)
```

```python
import functools
import math

import jax
import jax.numpy as jnp
from jax import lax
from jax.experimental import pallas as pl
from jax.experimental.pallas import tpu as pltpu

F32 = jnp.float32
BF16 = jnp.bfloat16

D_MODEL = 1024
ROPE_THETA = 500000.0
NORM_EPS = 1e-6

MLA_HEADS = 8
MLA_NOPE = 64
MLA_ROPE = 32
MLA_V = 64
MLA_Q_RANK = 384
MLA_KV_RANK = 256

HEAD_DIM = 64
PARTIAL_ROT_DIM = HEAD_DIM // 4
DIL_CONFIGS = ((128, 1), (512, 4), (2048, 16))
DIL_GROUPS = len(DIL_CONFIGS)
DIL_HEADS = 6
DIFF_HEADS = 4
D_FF = 2816
CONV_WIDTH = 3

GATE_COLS = 3 * D_MODEL
MLA_COLS = MLA_Q_RANK + MLA_KV_RANK + MLA_ROPE
DIL_COLS = DIL_GROUPS * 3 * DIL_HEADS * HEAD_DIM
DIFF_COLS = 3 * DIFF_HEADS * 2 * HEAD_DIM
DIL_OUT = DIL_HEADS * HEAD_DIM
DIFF_OUT = DIFF_HEADS * 2 * HEAD_DIM

LANES = 128
LOG2E = 1.4426950408889634
NEG_BIG = -1e30
VMEM_LIMIT = 56 * 1024 * 1024

TM_PROJ = 512
TM_MERGE = 256
TM_FFN = 512
TF_FFN = 1408
FFN_HALO = 16
TQ_MLA = 512
TQ_DIFF = 256
TK_MLA = 1024
TK_DIFF = 512
TJ_DIL = 512
DIL_RADIUS = 64
DIL_QSUB = 128
DIL_KWIN = DIL_QSUB + 2 * DIL_RADIUS


def _cparams(sem):
    return pltpu.CompilerParams(dimension_semantics=sem, vmem_limit_bytes=VMEM_LIMIT)


def _rms(x, g):
    ms = jnp.mean(x * x, axis=-1, keepdims=True)
    return x * lax.rsqrt(ms + NORM_EPS) * g


def _rope128(z, c, sa, sb, shift):
    return z * c + pltpu.roll(z, LANES - shift, 1) * sa + pltpu.roll(z, shift, 1) * sb


def _inproj_kernel(x_ref, g_ref, w_ref, *rest, rope_chunks, n_chunks):
    if rope_chunks:
        c_ref, sa_ref, sb_ref, o_ref = rest
    else:
        (o_ref,) = rest
    h = _rms(x_ref[...], g_ref[...]).astype(BF16)
    y = jnp.dot(h, w_ref[...], preferred_element_type=F32)
    for c in range(n_chunks):
        yc = y[:, c * LANES:(c + 1) * LANES]
        if c < rope_chunks:
            yc = _rope128(yc, c_ref[...], sa_ref[...], sb_ref[...], PARTIAL_ROT_DIM // 2)
        o_ref[:, c * LANES:(c + 1) * LANES] = yc.astype(o_ref.dtype)


def _inproj(x2, g, w, tables, *, tn, rope_chunks):
    t, d = x2.shape
    n = w.shape[1]
    tm = TM_PROJ
    in_specs = [
        pl.BlockSpec((tm, d), lambda j, i: (i, 0)),
        pl.BlockSpec((1, d), lambda j, i: (0, 0)),
        pl.BlockSpec((d, tn), lambda j, i: (0, j)),
    ]
    args = [x2, g, w]
    if rope_chunks:
        in_specs += [pl.BlockSpec((tm, LANES), lambda j, i: (i, 0))] * 3
        args += list(tables)
    return pl.pallas_call(
        functools.partial(_inproj_kernel, rope_chunks=rope_chunks, n_chunks=tn // LANES),
        grid=(n // tn, t // tm),
        in_specs=in_specs,
        out_specs=pl.BlockSpec((tm, tn), lambda j, i: (i, j)),
        out_shape=jax.ShapeDtypeStruct((t, n), BF16),
        name="inproj_rope" if rope_chunks else "inproj_gates",
        compiler_params=_cparams(("parallel", "parallel")),
    )(*args)


def _mla_prep_kernel(x_ref, g_ref, win_ref, gq_ref, gkv_ref, wuq_ref, wuk_ref, wuv_ref,
                     c_ref, sa_ref, sb_ref, q_ref, k_ref, v_ref):
    h = _rms(x_ref[...], g_ref[...]).astype(BF16)
    y = jnp.dot(h, win_ref[...], preferred_element_type=F32)
    cq = y[:, :MLA_Q_RANK]
    ckv = y[:, MLA_Q_RANK:MLA_Q_RANK + MLA_KV_RANK]
    kr = y[:, MLA_Q_RANK + MLA_KV_RANK:]
    qn = _rms(cq, gq_ref[...]).astype(BF16)
    kvn = _rms(ckv, gkv_ref[...]).astype(BF16)
    q = jnp.dot(qn, wuq_ref[...], preferred_element_type=F32)
    kn = jnp.dot(kvn, wuk_ref[...], preferred_element_type=F32)
    vv = jnp.dot(kvn, wuv_ref[...], preferred_element_type=F32)
    c, sa, sb = c_ref[...], sa_ref[...], sb_ref[...]
    krr = _rope128(kr, c, sa, sb, MLA_ROPE // 2)
    lane = lax.broadcasted_iota(jnp.int32, krr.shape, 1)
    for hd in range(MLA_HEADS):
        sl = slice(hd * LANES, (hd + 1) * LANES)
        q_ref[:, sl] = _rope128(q[:, sl], c, sa, sb, MLA_ROPE // 2).astype(BF16)
        k_ref[:, sl] = jnp.where(lane < MLA_NOPE, kn[:, sl], krr).astype(BF16)
        v_ref[:, sl] = jnp.where(lane < MLA_V, vv[:, sl], 1.0).astype(BF16)


def _mla_prep(x2, g, win, gq, gkv, wuq, wuk, wuv, tables):
    t, d = x2.shape
    tm = TM_PROJ
    hw = MLA_HEADS * LANES
    full = lambda a: pl.BlockSpec(a.shape, lambda i: (0,) * a.ndim)
    row = lambda w_: pl.BlockSpec((tm, w_), lambda i: (i, 0))
    out = jax.ShapeDtypeStruct((t, hw), BF16)
    return pl.pallas_call(
        _mla_prep_kernel,
        grid=(t // tm,),
        in_specs=[row(d), full(g), full(win), full(gq), full(gkv), full(wuq), full(wuk), full(wuv),
                  row(LANES), row(LANES), row(LANES)],
        out_specs=[row(hw)] * 3,
        out_shape=[out] * 3,
        name="mla_prep",
        compiler_params=_cparams(("parallel",)),
    )(x2, g, win, gq, gkv, wuq, wuk, wuv, *tables)


def _flash_sweep(q, k_at, v_at, m_ref, acc_ref, *, seq, tk):
    nv = acc_ref.shape[1]
    m_ref[...] = jnp.full(m_ref.shape, -jnp.inf, F32)
    acc_ref[...] = jnp.zeros(acc_ref.shape, F32)
    for c in range(seq // tk):
        sl = slice(c * tk, (c + 1) * tk)
        s = lax.dot_general(q, k_at(sl), (((1,), (1,)), ((), ())), preferred_element_type=F32)
        m_prev = m_ref[...]
        m_new = jnp.maximum(m_prev, jnp.max(s, axis=1, keepdims=True))
        alpha = jnp.exp2(m_prev - m_new)
        p = jnp.exp2((s - jnp.tile(m_new, (1, tk // LANES))).astype(BF16))
        acc_ref[...] = jnp.tile(alpha, (1, nv // LANES)) * acc_ref[...] + jnp.dot(
            p, v_at(sl), preferred_element_type=F32)
        m_ref[...] = m_new


def _flash_mla_kernel(q_ref, k_ref, v_ref, o_ref, m_ref, acc_ref, *, tk):
    _flash_sweep(q_ref[0], lambda sl: k_ref[0, sl, :], lambda sl: v_ref[0, sl, :], m_ref, acc_ref,
                 seq=k_ref.shape[1], tk=tk)
    acc = acc_ref[...]
    lane = lax.broadcasted_iota(jnp.int32, acc.shape, 1)
    o_ref[0] = jnp.where(lane < MLA_V, acc / pltpu.roll(acc, MLA_V, 1), 0.0).astype(o_ref.dtype)


def _flash_mla(q, k, v):
    b, s, hw = q.shape
    nh = hw // LANES
    tq, tk = min(TQ_MLA, s), min(TK_MLA, s)
    qspec = pl.BlockSpec((1, tq, LANES), lambda bi, h, i: (bi, i, h))
    kspec = pl.BlockSpec((1, s, LANES), lambda bi, h, i: (bi, 0, h))
    return pl.pallas_call(
        functools.partial(_flash_mla_kernel, tk=tk),
        grid=(b, nh, s // tq),
        in_specs=[qspec, kspec, kspec],
        out_specs=qspec,
        out_shape=jax.ShapeDtypeStruct((b, s, hw), BF16),
        scratch_shapes=[pltpu.VMEM((tq, LANES), F32)] * 2,
        name="flash_mla",
        compiler_params=_cparams(("parallel", "parallel", "arbitrary")),
    )(q, k, v)


def _flash_diff_kernel(q_ref, k_ref, v_ref, lam_ref, subln_ref, o_ref, v1_ref, m_ref, acc_ref, *, tk, lam_init):
    @pl.when(pl.program_id(2) == 0)
    def _():
        v1_ref[:, :LANES] = v_ref[0]
        v1_ref[:, LANES:] = jnp.ones((v1_ref.shape[0], LANES), BF16)

    q = q_ref[0]
    tq = q.shape[0]
    lane = lax.broadcasted_iota(jnp.int32, q.shape, 1)
    zero = jnp.zeros_like(q)
    q2 = jnp.concatenate([jnp.where(lane < HEAD_DIM, q, zero), jnp.where(lane >= HEAD_DIM, q, zero)], axis=0)
    _flash_sweep(q2, lambda sl: k_ref[0, sl, :], lambda sl: v1_ref[sl, :], m_ref, acc_ref,
                 seq=k_ref.shape[1], tk=tk)
    acc = acc_ref[...]
    o12 = acc[:, :LANES] / acc[:, LANES:]
    lv = lam_ref[...]
    lam = (jnp.exp(jnp.sum(lv[0:1] * lv[1:2], axis=1, keepdims=True))
           - jnp.exp(jnp.sum(lv[2:3] * lv[3:4], axis=1, keepdims=True)) + lam_init)
    o = o12[:tq] - lam * o12[tq:]
    o_ref[0] = (_rms(o, subln_ref[...]) * (1.0 - lam_init)).astype(o_ref.dtype)


def _flash_diff(proj, lam_vecs, subln, lam_init):
    b, s, _ = proj.shape
    nh = DIFF_HEADS
    tq, tk = min(TQ_DIFF, s), min(TK_DIFF, s)
    return pl.pallas_call(
        functools.partial(_flash_diff_kernel, tk=tk, lam_init=lam_init),
        grid=(b, nh, s // tq),
        in_specs=[
            pl.BlockSpec((1, tq, LANES), lambda bi, h, i: (bi, i, h)),
            pl.BlockSpec((1, s, LANES), lambda bi, h, i: (bi, 0, nh + h)),
            pl.BlockSpec((1, s, LANES), lambda bi, h, i: (bi, 0, 2 * nh + h)),
            pl.BlockSpec(lam_vecs.shape, lambda bi, h, i: (0, 0)),
            pl.BlockSpec(subln.shape, lambda bi, h, i: (0, 0)),
        ],
        out_specs=pl.BlockSpec((1, tq, LANES), lambda bi, h, i: (bi, i, h)),
        out_shape=jax.ShapeDtypeStruct((b, s, DIFF_OUT), BF16),
        scratch_shapes=[pltpu.VMEM((s, 2 * LANES), BF16), pltpu.VMEM((2 * tq, LANES), F32),
                        pltpu.VMEM((2 * tq, 2 * LANES), F32)],
        name="flash_diff",
        compiler_params=_cparams(("parallel", "parallel", "arbitrary")),
    )(proj, proj, proj, lam_vecs, subln)


def _dil_kernel(q_ref, kp_ref, kc_ref, kn_ref, vp_ref, vc_ref, vn_ref, o_ref, lse_ref, *, tj, lsub):
    j0 = pl.program_id(2) * tj
    r = DIL_RADIUS
    row = lax.broadcasted_iota(jnp.int32, (DIL_QSUB, DIL_KWIN), 0)
    col = lax.broadcasted_iota(jnp.int32, (DIL_QSUB, DIL_KWIN), 1)
    band = jnp.abs(col - r - row) <= r
    lane = lax.broadcasted_iota(jnp.int32, (DIL_QSUB, LANES), 1)
    low = lane < HEAD_DIM
    for c in range(DIL_OUT // LANES):
        sl = slice(c * LANES, (c + 1) * LANES)
        kcat = jnp.concatenate([kp_ref[0, tj - r:tj, sl], kc_ref[0, :, sl], kn_ref[0, 0:r, sl]], axis=0)
        vcat = jnp.concatenate([vp_ref[0, tj - r:tj, sl], vc_ref[0, :, sl], vn_ref[0, 0:r, sl]], axis=0)
        for i in range(tj // DIL_QSUB):
            kpos = j0 + (i * DIL_QSUB - r) + col
            valid = band & (kpos >= 0) & (kpos < lsub)
            q = q_ref[0, i * DIL_QSUB:(i + 1) * DIL_QSUB, sl]
            kw = kcat[i * DIL_QSUB:i * DIL_QSUB + DIL_KWIN]
            vw = vcat[i * DIL_QSUB:i * DIL_QSUB + DIL_KWIN]
            outs, lses = [], []
            for hh in range(2):
                qh = jnp.where(low if hh == 0 else ~low, q, jnp.zeros_like(q))
                s = lax.dot_general(qh, kw, (((1,), (1,)), ((), ())), preferred_element_type=F32)
                s = jnp.where(valid, s, NEG_BIG)
                m = jnp.max(s, axis=1, keepdims=True)
                p = jnp.exp2(s - m)
                l = jnp.sum(p, axis=1, keepdims=True)
                pv = jnp.dot(p.astype(BF16), vw, preferred_element_type=F32)
                outs.append(pv / l)
                lses.append(jnp.broadcast_to(m + jnp.log2(l), (DIL_QSUB, LANES)))
            rows = slice(i * DIL_QSUB, (i + 1) * DIL_QSUB)
            o_ref[0, rows, sl] = jnp.where(low, outs[0], outs[1])
            lse_ref[0, rows, sl] = jnp.where(low, lses[0], lses[1])


def _dilated(proj, g, dil):
    b, s, ncol = proj.shape
    lsub = s // dil
    tj = min(TJ_DIL, lsub)
    nblk = lsub // tj
    w = DIL_OUT
    per_tok = ncol // w
    view = proj.reshape(b, lsub, dil * ncol)

    def spec(t, shift):
        def imap(bi, ri, ji):
            jj = jnp.clip(ji + shift, 0, nblk - 1)
            return (bi, jj, ri * per_tok + g * 3 + t)
        return pl.BlockSpec((1, tj, w), imap)

    ospec = pl.BlockSpec((1, tj, w), lambda bi, ri, ji: (bi, ji, ri))
    oshape = jax.ShapeDtypeStruct((b, lsub, dil * w), F32)
    o, lse = pl.pallas_call(
        functools.partial(_dil_kernel, tj=tj, lsub=lsub),
        grid=(b, dil, nblk),
        in_specs=[spec(0, 0), spec(1, -1), spec(1, 0), spec(1, 1), spec(2, -1), spec(2, 0), spec(2, 1)],
        out_specs=[ospec, ospec],
        out_shape=[oshape, oshape],
        name=f"dilated_d{dil}",
        compiler_params=_cparams(("parallel", "parallel", "parallel")),
    )(view, view, view, view, view, view, view)
    return o.reshape(b * s, w), lse.reshape(b * s, w)


def _merge_kernel(x_ref, gate_ref, omla_ref, od0, od1, od2, ls0, ls1, ls2, odiff_ref,
                  wbm_ref, wbd_ref, wbf_ref, wo_ref, pg_ref, o_ref):
    l0, l1, l2 = ls0[...], ls1[...], ls2[...]
    mx = jnp.maximum(jnp.maximum(l0, l1), l2)
    e0, e1, e2 = jnp.exp2(l0 - mx), jnp.exp2(l1 - mx), jnp.exp2(l2 - mx)
    odil = ((e0 * od0[...] + e1 * od1[...] + e2 * od2[...]) / (e0 + e1 + e2)).astype(BF16)
    d = D_MODEL
    y = jax.nn.sigmoid(gate_ref[:, 0:d].astype(F32)) * jnp.dot(omla_ref[...], wbm_ref[...], preferred_element_type=F32)
    y = y + jax.nn.sigmoid(gate_ref[:, d:2 * d].astype(F32)) * jnp.dot(odil, wbd_ref[...], preferred_element_type=F32)
    y = y + jax.nn.sigmoid(gate_ref[:, 2 * d:3 * d].astype(F32)) * jnp.dot(odiff_ref[...], wbf_ref[...], preferred_element_type=F32)
    z = jnp.dot(y.astype(BF16), wo_ref[...], preferred_element_type=F32)
    o_ref[...] = x_ref[...] + _rms(z, pg_ref[...])


def _merge(x2, gates, omla, odil, lses, odiff, wbm, wbd, wbf, wo, pg):
    t, d = x2.shape
    tm = TM_MERGE
    full = lambda a: pl.BlockSpec(a.shape, lambda i: (0,) * a.ndim)
    row = lambda a: pl.BlockSpec((tm, a.shape[1]), lambda i: (i, 0))
    acts = [x2, gates, omla, *odil, *lses, odiff]
    wts = [wbm, wbd, wbf, wo, pg]
    return pl.pallas_call(
        _merge_kernel,
        grid=(t // tm,),
        in_specs=[row(a) for a in acts] + [full(a) for a in wts],
        out_specs=pl.BlockSpec((tm, d), lambda i: (i, 0)),
        out_shape=jax.ShapeDtypeStruct((t, d), F32),
        name="merge",
        compiler_params=_cparams(("parallel",)),
    )(*acts, *wts)


def _ffn_kernel(xp_ref, x_ref, xn_ref, g_ref, wg_ref, wv_ref, cg_ref, cv_ref, wd_ref, pg_ref,
                o_ref, h_ref, acc_ref, *, tm, seq):
    i = pl.program_id(0)
    c = pl.program_id(1)
    hl = FFN_HALO

    @pl.when(c == 0)
    def _():
        g = g_ref[...]
        keep_p = ((i * tm) % seq != 0).astype(F32)
        keep_n = (((i + 1) * tm) % seq != 0).astype(F32)
        h_ref[0:hl, :] = (_rms(xp_ref[...], g) * keep_p).astype(BF16)
        h_ref[hl:hl + tm, :] = _rms(x_ref[...], g).astype(BF16)
        h_ref[hl + tm:2 * hl + tm, :] = (_rms(xn_ref[...], g) * keep_n).astype(BF16)
        acc_ref[...] = jnp.zeros(acc_ref.shape, F32)

    h = h_ref[...]
    n_ext = tm + 2 * hl

    def conv(u, cw):
        up = pltpu.roll(u, 1, 0)
        dn = pltpu.roll(u, n_ext - 1, 0)
        z = cw[0:1] * up + cw[1:2] * u + cw[2:3] * dn
        return z[hl:hl + tm]

    ug = conv(jnp.dot(h, wg_ref[...], preferred_element_type=F32), cg_ref[...])
    uv = conv(jnp.dot(h, wv_ref[...], preferred_element_type=F32), cv_ref[...])
    a = (jax.nn.gelu(ug, approximate=True) * uv).astype(BF16)
    acc_ref[...] += jnp.dot(a, wd_ref[...], preferred_element_type=F32)

    @pl.when(c == pl.num_programs(1) - 1)
    def _():
        o_ref[...] = x_ref[...] + _rms(acc_ref[...], pg_ref[...])


def _ffn(x2, g, wup, convw, wdown, pg, seq):
    t, d = x2.shape
    tm = min(TM_FFN, seq)
    tf = TF_FFN
    nch = D_FF // tf
    hl = FFN_HALO
    rb = tm // hl
    nhb = t // hl
    return pl.pallas_call(
        functools.partial(_ffn_kernel, tm=tm, seq=seq),
        grid=(t // tm, nch),
        in_specs=[
            pl.BlockSpec((hl, d), lambda i, c: (jnp.maximum(i * rb - 1, 0), 0)),
            pl.BlockSpec((tm, d), lambda i, c: (i, 0)),
            pl.BlockSpec((hl, d), lambda i, c: (jnp.minimum((i + 1) * rb, nhb - 1), 0)),
            pl.BlockSpec((1, d), lambda i, c: (0, 0)),
            pl.BlockSpec((d, tf), lambda i, c: (0, c)),
            pl.BlockSpec((d, tf), lambda i, c: (0, c + nch)),
            pl.BlockSpec((CONV_WIDTH, tf), lambda i, c: (0, c)),
            pl.BlockSpec((CONV_WIDTH, tf), lambda i, c: (0, c + nch)),
            pl.BlockSpec((tf, d), lambda i, c: (c, 0)),
            pl.BlockSpec((1, d), lambda i, c: (0, 0)),
        ],
        out_specs=pl.BlockSpec((tm, d), lambda i, c: (i, 0)),
        out_shape=jax.ShapeDtypeStruct((t, d), F32),
        scratch_shapes=[pltpu.VMEM((tm + 2 * hl, d), BF16), pltpu.VMEM((tm, d), F32)],
        name="conv_ffn",
        compiler_params=_cparams(("parallel", "arbitrary")),
    )(x2, x2, x2, g, wup, wup, convw, convw, wdown, pg)


def _rope_tables(positions, rot_dim, period, base):
    half = rot_dim // 2
    inv = ROPE_THETA ** (-jnp.arange(0, rot_dim, 2, dtype=F32) / rot_dim)
    ang = positions.astype(F32).reshape(-1, 1) * inv[None, :]
    cos, sin = jnp.cos(ang), jnp.sin(ang)
    lane = jnp.arange(LANES)
    rel = lane % period - base
    idx = jnp.clip(rel, 0, rot_dim - 1) % half
    first = (rel >= 0) & (rel < half)
    second = (rel >= half) & (rel < rot_dim)
    cg, sg = cos[:, idx], sin[:, idx]
    c = jnp.where((first | second)[None, :], cg, 1.0)
    sa = jnp.where(first[None, :], -sg, 0.0)
    sb = jnp.where(second[None, :], sg, 0.0)
    return c, sa, sb


def _layer_weights(l, w_in, mla_w_uq, mla_w_ukv, w_branch_mla):
    d = D_MODEL
    wi = w_in[l]
    o1 = GATE_COLS
    o2 = o1 + MLA_COLS
    o3 = o2 + DIL_COLS
    w_gate = wi[:, :o1].astype(BF16)
    wm = wi[:, o1:o2]
    z = lambda n: jnp.zeros((d, n), F32)
    w_mla_in = jnp.concatenate(
        [wm[:, :MLA_Q_RANK + MLA_KV_RANK], z(MLA_NOPE), wm[:, MLA_Q_RANK + MLA_KV_RANK:], z(LANES - MLA_NOPE - MLA_ROPE)],
        axis=1).astype(BF16)
    qscale = LOG2E * HEAD_DIM ** -0.5
    wd = wi[:, o2:o3].reshape(d, DIL_GROUPS, 3, DIL_OUT)
    wd = wd * jnp.array([qscale, 1.0, 1.0], F32)[None, None, :, None]
    w_dil = wd.reshape(d, DIL_COLS).astype(BF16)
    wf = wi[:, o3:].reshape(d, 3, DIFF_OUT)
    wf = wf * jnp.array([qscale, 1.0, 1.0], F32)[None, :, None]
    w_diff = wf.reshape(d, DIFF_COLS).astype(BF16)
    qk = MLA_NOPE + MLA_ROPE
    wq = mla_w_uq[l].reshape(MLA_Q_RANK, MLA_HEADS, qk) * (LOG2E * qk ** -0.5)
    wq = jnp.pad(wq, ((0, 0), (0, 0), (0, LANES - qk))).reshape(MLA_Q_RANK, MLA_HEADS * LANES).astype(BF16)
    wkv = mla_w_ukv[l].reshape(MLA_KV_RANK, MLA_HEADS, MLA_NOPE + MLA_V)
    pad_head = lambda w_: jnp.pad(w_, ((0, 0), (0, 0), (0, LANES - w_.shape[2]))).reshape(
        MLA_KV_RANK, MLA_HEADS * LANES).astype(BF16)
    wk = pad_head(wkv[:, :, :MLA_NOPE])
    wv = pad_head(wkv[:, :, MLA_NOPE:])
    wb = w_branch_mla[l].reshape(MLA_HEADS, MLA_V, d)
    wb = jnp.pad(wb, ((0, 0), (0, LANES - MLA_V), (0, 0))).reshape(MLA_HEADS * LANES, d).astype(BF16)
    return w_gate, w_mla_in, w_dil, w_diff, wq, wk, wv, wb


def _lambda_init(layer):
    return 0.8 - 0.6 * math.exp(-0.3 * layer)


def kernel(x, positions, attn_pre_norm, w_in, mla_q_norm, mla_w_uq, mla_kv_norm, mla_w_ukv, diff_lambda,
           diff_subln, w_branch_mla, w_branch_dil, w_branch_diff, w_out, attn_post_norm, ffn_pre_norm,
           w_up, ffn_conv, w_down, ffn_post_norm):
    b, s, d = x.shape
    t = b * s
    depth = w_in.shape[0]
    tab_p = _rope_tables(positions, PARTIAL_ROT_DIM, HEAD_DIM, 0)
    tab_m = _rope_tables(positions, MLA_ROPE, LANES, MLA_NOPE)
    x2 = x.reshape(t, d)
    row = lambda a: a.reshape(1, -1)
    for l in range(depth):
        w_gate, w_mla_in, w_dil, w_diff, wq, wk, wv, wbm = _layer_weights(l, w_in, mla_w_uq, mla_w_ukv, w_branch_mla)
        g_pre = row(attn_pre_norm[l])
        gates = _inproj(x2, g_pre, w_gate, None, tn=GATE_COLS // 2, rope_chunks=0)
        p_dil = _inproj(x2, g_pre, w_dil, tab_p, tn=DIL_COLS // DIL_GROUPS, rope_chunks=2 * DIL_OUT // LANES)
        p_diff = _inproj(x2, g_pre, w_diff, tab_p, tn=DIFF_COLS, rope_chunks=2 * DIFF_OUT // LANES)
        q_m, k_m, v_m = _mla_prep(x2, g_pre, w_mla_in, row(mla_q_norm[l]), row(mla_kv_norm[l]), wq, wk, wv, tab_m)
        hw = MLA_HEADS * LANES
        o_mla = _flash_mla(q_m.reshape(b, s, hw), k_m.reshape(b, s, hw), v_m.reshape(b, s, hw)).reshape(t, hw)
        o_diff = _flash_diff(p_diff.reshape(b, s, DIFF_COLS), diff_lambda[l], row(diff_subln[l]),
                             _lambda_init(l)).reshape(t, DIFF_OUT)
        p_dil3 = p_dil.reshape(b, s, DIL_COLS)
        dil = [_dilated(p_dil3, gi, dl) for gi, (_, dl) in enumerate(DIL_CONFIGS)]
        x2 = _merge(x2, gates, o_mla, [o for o, _ in dil], [ls for _, ls in dil], o_diff,
                    wbm, w_branch_dil[l].astype(BF16), w_branch_diff[l].astype(BF16), w_out[l].astype(BF16),
                    row(attn_post_norm[l]))
        x2 = _ffn(x2, row(ffn_pre_norm[l]), w_up[l].astype(BF16), ffn_conv[l], w_down[l].astype(BF16),
                  row(ffn_post_norm[l]), s)
    return x2.reshape(b, s, d)
```

```python
import functools
import math

import jax
import jax.numpy as jnp
from jax import lax
from jax.experimental import pallas as pl
from jax.experimental.pallas import tpu as pltpu

F32 = jnp.float32
BF16 = jnp.bfloat16

D_MODEL = 1024
ROPE_THETA = 500000.0
NORM_EPS = 1e-6

MLA_HEADS = 8
MLA_NOPE = 64
MLA_ROPE = 32
MLA_V = 64
MLA_Q_RANK = 384
MLA_KV_RANK = 256

HEAD_DIM = 64
PARTIAL_ROT_DIM = HEAD_DIM // 4
DIL_CONFIGS = ((128, 1), (512, 4), (2048, 16))
DIL_GROUPS = len(DIL_CONFIGS)
DIL_HEADS = 6
DIFF_HEADS = 4
D_FF = 2816
CONV_WIDTH = 3

GATE_COLS = 3 * D_MODEL
MLA_COLS = MLA_Q_RANK + MLA_KV_RANK + MLA_ROPE
DIL_COLS = DIL_GROUPS * 3 * DIL_HEADS * HEAD_DIM
DIFF_COLS = 3 * DIFF_HEADS * 2 * HEAD_DIM
DIL_OUT = DIL_HEADS * HEAD_DIM
DIFF_OUT = DIFF_HEADS * 2 * HEAD_DIM

LANES = 128
LOG2E = 1.4426950408889634
NEG_BIG = -1e30
VMEM_LIMIT = 56 * 1024 * 1024

TM_PROJ = 512
TM_MERGE = 256
TM_FFN = 512
TF_FFN = 1408
FFN_HALO = 16
TQ_MLA = 1024
TQ_DIFF = 512
TK_MLA = 1024
TK_DIFF = 512
TJ_DIL = 512
DIL_RADIUS = 64
DIL_QSUB = 128
DIL_KWIN = DIL_QSUB + 2 * DIL_RADIUS


def _cparams(sem):
    return pltpu.CompilerParams(dimension_semantics=sem, vmem_limit_bytes=VMEM_LIMIT)


def _rms(x, g):
    ms = jnp.mean(x * x, axis=-1, keepdims=True)
    return x * lax.rsqrt(ms + NORM_EPS) * g


def _rope128(z, c, sa, sb, shift):
    return z * c + pltpu.roll(z, LANES - shift, 1) * sa + pltpu.roll(z, shift, 1) * sb


def _inproj_kernel(x_ref, g_ref, w_ref, *rest, rope_chunks, n_chunks, dil):
    rest = list(rest)
    scr_ref = rest.pop() if dil > 1 else None
    o_ref = rest.pop()
    h = _rms(x_ref[...], g_ref[...]).astype(BF16)
    y = jnp.dot(h, w_ref[...], preferred_element_type=F32)
    for c in range(n_chunks):
        yc = y[:, c * LANES:(c + 1) * LANES]
        if c < rope_chunks:
            c_ref, sa_ref, sb_ref = rest
            yc = _rope128(yc, c_ref[...], sa_ref[...], sb_ref[...], PARTIAL_ROT_DIM // 2)
        if dil > 1:
            scr_ref[c] = yc
        else:
            o_ref[:, c * LANES:(c + 1) * LANES] = yc.astype(o_ref.dtype)
    if dil > 1:
        rows = y.shape[0] // dil
        for r in range(dil):
            for c in range(n_chunks):
                col = (r * n_chunks + c) * LANES
                o_ref[:, col:col + LANES] = scr_ref[c, pl.ds(r, rows, stride=dil), :].astype(o_ref.dtype)


def _inproj(x2, g, w, tables, *, tn, rope_chunks, dil=1):
    t, d = x2.shape
    n = w.shape[1]
    tm = TM_PROJ
    assert dil == 1 or n == tn
    in_specs = [
        pl.BlockSpec((tm, d), lambda j, i: (i, 0)),
        pl.BlockSpec((1, d), lambda j, i: (0, 0)),
        pl.BlockSpec((d, tn), lambda j, i: (0, j)),
    ]
    args = [x2, g, w]
    if rope_chunks:
        in_specs += [pl.BlockSpec((tm, LANES), lambda j, i: (i, 0))] * 3
        args += list(tables)
    if dil > 1:
        out_spec = pl.BlockSpec((tm // dil, dil * tn), lambda j, i: (i, 0))
        out_shape = jax.ShapeDtypeStruct((t // dil, dil * tn), BF16)
        scratch = [pltpu.VMEM((tn // LANES, tm, LANES), F32)]
    else:
        out_spec = pl.BlockSpec((tm, tn), lambda j, i: (i, j))
        out_shape = jax.ShapeDtypeStruct((t, n), BF16)
        scratch = []
    return pl.pallas_call(
        functools.partial(_inproj_kernel, rope_chunks=rope_chunks, n_chunks=tn // LANES, dil=dil),
        grid=(n // tn, t // tm),
        in_specs=in_specs,
        out_specs=out_spec,
        out_shape=out_shape,
        scratch_shapes=scratch,
        name=f"inproj_rope_d{dil}" if rope_chunks else "inproj_gates",
        compiler_params=_cparams(("parallel", "parallel")),
    )(*args)


def _mla_prep_kernel(x_ref, g_ref, win_ref, gq_ref, gkv_ref, wuq_ref, wuk_ref, wuv_ref,
                     c_ref, sa_ref, sb_ref, q_ref, k_ref, v_ref):
    h = _rms(x_ref[...], g_ref[...]).astype(BF16)
    y = jnp.dot(h, win_ref[...], preferred_element_type=F32)
    cq = y[:, :MLA_Q_RANK]
    ckv = y[:, MLA_Q_RANK:MLA_Q_RANK + MLA_KV_RANK]
    kr = y[:, MLA_Q_RANK + MLA_KV_RANK:]
    qn = _rms(cq, gq_ref[...]).astype(BF16)
    kvn = _rms(ckv, gkv_ref[...]).astype(BF16)
    q = jnp.dot(qn, wuq_ref[...], preferred_element_type=F32)
    kn = jnp.dot(kvn, wuk_ref[...], preferred_element_type=F32)
    vv = jnp.dot(kvn, wuv_ref[...], preferred_element_type=F32)
    c, sa, sb = c_ref[...], sa_ref[...], sb_ref[...]
    krr = _rope128(kr, c, sa, sb, MLA_ROPE // 2)
    lane = lax.broadcasted_iota(jnp.int32, krr.shape, 1)
    for hd in range(MLA_HEADS):
        sl = slice(hd * LANES, (hd + 1) * LANES)
        q_ref[:, sl] = _rope128(q[:, sl], c, sa, sb, MLA_ROPE // 2).astype(BF16)
        k_ref[:, sl] = jnp.where(lane < MLA_NOPE, kn[:, sl], krr).astype(BF16)
        v_ref[:, sl] = jnp.where(lane < MLA_V, vv[:, sl], 1.0).astype(BF16)


def _mla_prep(x2, g, win, gq, gkv, wuq, wuk, wuv, tables):
    t, d = x2.shape
    tm = TM_PROJ
    hw = MLA_HEADS * LANES
    full = lambda a: pl.BlockSpec(a.shape, lambda i: (0,) * a.ndim)
    row = lambda w_: pl.BlockSpec((tm, w_), lambda i: (i, 0))
    out = jax.ShapeDtypeStruct((t, hw), BF16)
    return pl.pallas_call(
        _mla_prep_kernel,
        grid=(t // tm,),
        in_specs=[row(d), full(g), full(win), full(gq), full(gkv), full(wuq), full(wuk), full(wuv),
                  row(LANES), row(LANES), row(LANES)],
        out_specs=[row(hw)] * 3,
        out_shape=[out] * 3,
        name="mla_prep",
        compiler_params=_cparams(("parallel",)),
    )(x2, g, win, gq, gkv, wuq, wuk, wuv, *tables)


def _flash_sweep(q, k_at, v_at, m_ref, acc_ref, *, seq, tk):
    nv = acc_ref.shape[1]
    m_ref[...] = jnp.full(m_ref.shape, -jnp.inf, F32)
    acc_ref[...] = jnp.zeros(acc_ref.shape, F32)
    for c in range(seq // tk):
        sl = slice(c * tk, (c + 1) * tk)
        s = lax.dot_general(q, k_at(sl), (((1,), (1,)), ((), ())), preferred_element_type=F32)
        m_prev = m_ref[...]
        m_new = jnp.maximum(m_prev, jnp.max(s, axis=1, keepdims=True))
        alpha = jnp.exp2(m_prev - m_new)
        p = jnp.exp2((s - jnp.tile(m_new, (1, tk // LANES))).astype(BF16))
        acc_ref[...] = jnp.tile(alpha, (1, nv // LANES)) * acc_ref[...] + jnp.dot(
            p, v_at(sl), preferred_element_type=F32)
        m_ref[...] = m_new


def _flash_mla_kernel(q_ref, k_ref, v_ref, o_ref, m_ref, acc_ref, *, tk):
    _flash_sweep(q_ref[0], lambda sl: k_ref[0, sl, :], lambda sl: v_ref[0, sl, :], m_ref, acc_ref,
                 seq=k_ref.shape[1], tk=tk)
    acc = acc_ref[...]
    lane = lax.broadcasted_iota(jnp.int32, acc.shape, 1)
    o_ref[0] = jnp.where(lane < MLA_V, acc / pltpu.roll(acc, MLA_V, 1), 0.0).astype(o_ref.dtype)


def _flash_mla(q, k, v):
    b, s, hw = q.shape
    nh = hw // LANES
    tq, tk = min(TQ_MLA, s), min(TK_MLA, s)
    qspec = pl.BlockSpec((1, tq, LANES), lambda bi, h, i: (bi, i, h))
    kspec = pl.BlockSpec((1, s, LANES), lambda bi, h, i: (bi, 0, h))
    return pl.pallas_call(
        functools.partial(_flash_mla_kernel, tk=tk),
        grid=(b, nh, s // tq),
        in_specs=[qspec, kspec, kspec],
        out_specs=qspec,
        out_shape=jax.ShapeDtypeStruct((b, s, hw), BF16),
        scratch_shapes=[pltpu.VMEM((tq, LANES), F32)] * 2,
        name="flash_mla",
        compiler_params=_cparams(("parallel", "parallel", "arbitrary")),
    )(q, k, v)


def _flash_diff_kernel(q_ref, k_ref, v_ref, lam_ref, subln_ref, o_ref, v1_ref, m_ref, acc_ref, *, tk, lam_init):
    @pl.when(pl.program_id(2) == 0)
    def _():
        v1_ref[:, :LANES] = v_ref[0]
        v1_ref[:, LANES:] = jnp.ones((v1_ref.shape[0], LANES), BF16)

    q = q_ref[0]
    tq = q.shape[0]
    lane = lax.broadcasted_iota(jnp.int32, q.shape, 1)
    zero = jnp.zeros_like(q)
    q2 = jnp.concatenate([jnp.where(lane < HEAD_DIM, q, zero), jnp.where(lane >= HEAD_DIM, q, zero)], axis=0)
    _flash_sweep(q2, lambda sl: k_ref[0, sl, :], lambda sl: v1_ref[sl, :], m_ref, acc_ref,
                 seq=k_ref.shape[1], tk=tk)
    acc = acc_ref[...]
    o12 = acc[:, :LANES] / acc[:, LANES:]
    lv = lam_ref[...]
    lam = (jnp.exp(jnp.sum(lv[0:1] * lv[1:2], axis=1, keepdims=True))
           - jnp.exp(jnp.sum(lv[2:3] * lv[3:4], axis=1, keepdims=True)) + lam_init)
    o = o12[:tq] - lam * o12[tq:]
    o_ref[0] = (_rms(o, subln_ref[...]) * (1.0 - lam_init)).astype(o_ref.dtype)


def _flash_diff(proj, lam_vecs, subln, lam_init):
    b, s, _ = proj.shape
    nh = DIFF_HEADS
    tq, tk = min(TQ_DIFF, s), min(TK_DIFF, s)
    return pl.pallas_call(
        functools.partial(_flash_diff_kernel, tk=tk, lam_init=lam_init),
        grid=(b, nh, s // tq),
        in_specs=[
            pl.BlockSpec((1, tq, LANES), lambda bi, h, i: (bi, i, h)),
            pl.BlockSpec((1, s, LANES), lambda bi, h, i: (bi, 0, nh + h)),
            pl.BlockSpec((1, s, LANES), lambda bi, h, i: (bi, 0, 2 * nh + h)),
            pl.BlockSpec(lam_vecs.shape, lambda bi, h, i: (0, 0)),
            pl.BlockSpec(subln.shape, lambda bi, h, i: (0, 0)),
        ],
        out_specs=pl.BlockSpec((1, tq, LANES), lambda bi, h, i: (bi, i, h)),
        out_shape=jax.ShapeDtypeStruct((b, s, DIFF_OUT), BF16),
        scratch_shapes=[pltpu.VMEM((s, 2 * LANES), BF16), pltpu.VMEM((2 * tq, LANES), F32),
                        pltpu.VMEM((2 * tq, 2 * LANES), F32)],
        name="flash_diff",
        compiler_params=_cparams(("parallel", "parallel", "arbitrary")),
    )(proj, proj, proj, lam_vecs, subln)


def _dil_kernel(q_ref, kp_ref, kc_ref, kn_ref, vp_ref, vc_ref, vn_ref, o_ref, lse_ref, *, tj, lsub):
    j0 = pl.program_id(2) * tj
    r = DIL_RADIUS
    row = lax.broadcasted_iota(jnp.int32, (DIL_QSUB, DIL_KWIN), 0)
    col = lax.broadcasted_iota(jnp.int32, (DIL_QSUB, DIL_KWIN), 1)
    band = jnp.abs(col - r - row) <= r
    lane = lax.broadcasted_iota(jnp.int32, (DIL_QSUB, LANES), 1)
    low = lane < HEAD_DIM
    for c in range(DIL_OUT // LANES):
        sl = slice(c * LANES, (c + 1) * LANES)
        kcat = jnp.concatenate([kp_ref[0, tj - r:tj, sl], kc_ref[0, :, sl], kn_ref[0, 0:r, sl]], axis=0)
        vcat = jnp.concatenate([vp_ref[0, tj - r:tj, sl], vc_ref[0, :, sl], vn_ref[0, 0:r, sl]], axis=0)
        for i in range(tj // DIL_QSUB):
            kpos = j0 + (i * DIL_QSUB - r) + col
            valid = band & (kpos >= 0) & (kpos < lsub)
            q = q_ref[0, i * DIL_QSUB:(i + 1) * DIL_QSUB, sl]
            kw = kcat[i * DIL_QSUB:i * DIL_QSUB + DIL_KWIN]
            vw = vcat[i * DIL_QSUB:i * DIL_QSUB + DIL_KWIN]
            outs, lses = [], []
            for hh in range(2):
                qh = jnp.where(low if hh == 0 else ~low, q, jnp.zeros_like(q))
                s = lax.dot_general(qh, kw, (((1,), (1,)), ((), ())), preferred_element_type=F32)
                s = jnp.where(valid, s, NEG_BIG)
                m = jnp.max(s, axis=1, keepdims=True)
                p = jnp.exp2(s - m)
                l = jnp.sum(p, axis=1, keepdims=True)
                pv = jnp.dot(p.astype(BF16), vw, preferred_element_type=F32)
                outs.append(pv / l)
                lses.append(jnp.broadcast_to(m + jnp.log2(l), (DIL_QSUB, LANES)))
            rows = slice(i * DIL_QSUB, (i + 1) * DIL_QSUB)
            o_ref[0, rows, sl] = jnp.where(low, outs[0], outs[1])
            lse_ref[0, rows, sl] = jnp.where(low, lses[0], lses[1])


def _dilated(proj, dil, b, s):
    w = DIL_OUT
    lsub = s // dil
    tj = min(TJ_DIL, lsub)
    nblk = lsub // tj
    view = proj.reshape(b, lsub, dil * 3 * w)

    def spec(t, shift):
        def imap(bi, ri, ji):
            jj = jnp.clip(ji + shift, 0, nblk - 1)
            return (bi, jj, ri * 3 + t)
        return pl.BlockSpec((1, tj, w), imap)

    ospec = pl.BlockSpec((1, tj, w), lambda bi, ri, ji: (bi, ji, ri))
    oshape = jax.ShapeDtypeStruct((b, lsub, dil * w), F32)
    o, lse = pl.pallas_call(
        functools.partial(_dil_kernel, tj=tj, lsub=lsub),
        grid=(b, dil, nblk),
        in_specs=[spec(0, 0), spec(1, -1), spec(1, 0), spec(1, 1), spec(2, -1), spec(2, 0), spec(2, 1)],
        out_specs=[ospec, ospec],
        out_shape=[oshape, oshape],
        name=f"dilated_d{dil}",
        compiler_params=_cparams(("parallel", "parallel", "parallel")),
    )(view, view, view, view, view, view, view)
    return o.reshape(b * lsub, dil * w), lse.reshape(b * lsub, dil * w)


def _merge_kernel(x_ref, gate_ref, omla_ref, od0, od1, od2, ls0, ls1, ls2, odiff_ref,
                  wbm_ref, wbd_ref, wbf_ref, wo_ref, pg_ref, o_ref, scr_ref):
    tm = x_ref.shape[0]
    nch = DIL_OUT // LANES

    def natural(ref, dil):
        if dil == 1:
            return ref[...]
        rows = tm // dil
        for r in range(dil):
            for c in range(nch):
                col = (r * nch + c) * LANES
                scr_ref[c, pl.ds(r, rows, stride=dil), :] = ref[:, col:col + LANES]
        return jnp.concatenate([scr_ref[c] for c in range(nch)], axis=1)

    dils = [dl for _, dl in DIL_CONFIGS]
    l0, l1, l2 = (natural(r_, dl) for r_, dl in zip((ls0, ls1, ls2), dils))
    mx = jnp.maximum(jnp.maximum(l0, l1), l2)
    e0, e1, e2 = jnp.exp2(l0 - mx), jnp.exp2(l1 - mx), jnp.exp2(l2 - mx)
    den = e0 + e1 + e2
    num = e0 * natural(od0, dils[0])
    num = num + e1 * natural(od1, dils[1])
    num = num + e2 * natural(od2, dils[2])
    odil = (num / den).astype(BF16)
    d = D_MODEL
    y = jax.nn.sigmoid(gate_ref[:, 0:d].astype(F32)) * jnp.dot(omla_ref[...], wbm_ref[...], preferred_element_type=F32)
    y = y + jax.nn.sigmoid(gate_ref[:, d:2 * d].astype(F32)) * jnp.dot(odil, wbd_ref[...], preferred_element_type=F32)
    y = y + jax.nn.sigmoid(gate_ref[:, 2 * d:3 * d].astype(F32)) * jnp.dot(odiff_ref[...], wbf_ref[...], preferred_element_type=F32)
    z = jnp.dot(y.astype(BF16), wo_ref[...], preferred_element_type=F32)
    o_ref[...] = x_ref[...] + _rms(z, pg_ref[...])


def _merge(x2, gates, omla, odil, lses, odiff, wbm, wbd, wbf, wo, pg):
    t, d = x2.shape
    tm = TM_MERGE
    full = lambda a: pl.BlockSpec(a.shape, lambda i: (0,) * a.ndim)
    row = lambda a: pl.BlockSpec((tm * a.shape[0] // t, a.shape[1]), lambda i: (i, 0))
    acts = [x2, gates, omla, *odil, *lses, odiff]
    wts = [wbm, wbd, wbf, wo, pg]
    return pl.pallas_call(
        _merge_kernel,
        grid=(t // tm,),
        in_specs=[row(a) for a in acts] + [full(a) for a in wts],
        out_specs=pl.BlockSpec((tm, d), lambda i: (i, 0)),
        out_shape=jax.ShapeDtypeStruct((t, d), F32),
        scratch_shapes=[pltpu.VMEM((DIL_OUT // LANES, tm, LANES), F32)],
        name="merge",
        compiler_params=_cparams(("parallel",)),
    )(*acts, *wts)


def _ffn_kernel(xp_ref, x_ref, xn_ref, g_ref, wg_ref, wv_ref, cg_ref, cv_ref, wd_ref, pg_ref,
                o_ref, h_ref, acc_ref, *, tm, seq):
    i = pl.program_id(0)
    c = pl.program_id(1)
    hl = FFN_HALO

    @pl.when(c == 0)
    def _():
        g = g_ref[...]
        keep_p = ((i * tm) % seq != 0).astype(F32)
        keep_n = (((i + 1) * tm) % seq != 0).astype(F32)
        h_ref[0:hl, :] = (_rms(xp_ref[...], g) * keep_p).astype(BF16)
        h_ref[hl:hl + tm, :] = _rms(x_ref[...], g).astype(BF16)
        h_ref[hl + tm:2 * hl + tm, :] = (_rms(xn_ref[...], g) * keep_n).astype(BF16)
        acc_ref[...] = jnp.zeros(acc_ref.shape, F32)

    h = h_ref[...]
    n_ext = tm + 2 * hl

    def conv(u, cw):
        up = pltpu.roll(u, 1, 0)
        dn = pltpu.roll(u, n_ext - 1, 0)
        z = cw[0:1] * up + cw[1:2] * u + cw[2:3] * dn
        return z[hl:hl + tm]

    ug = conv(jnp.dot(h, wg_ref[...], preferred_element_type=F32), cg_ref[...])
    uv = conv(jnp.dot(h, wv_ref[...], preferred_element_type=F32), cv_ref[...])
    a = (jax.nn.gelu(ug, approximate=True) * uv).astype(BF16)
    acc_ref[...] += jnp.dot(a, wd_ref[...], preferred_element_type=F32)

    @pl.when(c == pl.num_programs(1) - 1)
    def _():
        o_ref[...] = x_ref[...] + _rms(acc_ref[...], pg_ref[...])


def _ffn(x2, g, wup, convw, wdown, pg, seq):
    t, d = x2.shape
    tm = min(TM_FFN, seq)
    tf = TF_FFN
    nch = D_FF // tf
    hl = FFN_HALO
    rb = tm // hl
    nhb = t // hl
    return pl.pallas_call(
        functools.partial(_ffn_kernel, tm=tm, seq=seq),
        grid=(t // tm, nch),
        in_specs=[
            pl.BlockSpec((hl, d), lambda i, c: (jnp.maximum(i * rb - 1, 0), 0)),
            pl.BlockSpec((tm, d), lambda i, c: (i, 0)),
            pl.BlockSpec((hl, d), lambda i, c: (jnp.minimum((i + 1) * rb, nhb - 1), 0)),
            pl.BlockSpec((1, d), lambda i, c: (0, 0)),
            pl.BlockSpec((d, tf), lambda i, c: (0, c)),
            pl.BlockSpec((d, tf), lambda i, c: (0, c + nch)),
            pl.BlockSpec((CONV_WIDTH, tf), lambda i, c: (0, c)),
            pl.BlockSpec((CONV_WIDTH, tf), lambda i, c: (0, c + nch)),
            pl.BlockSpec((tf, d), lambda i, c: (c, 0)),
            pl.BlockSpec((1, d), lambda i, c: (0, 0)),
        ],
        out_specs=pl.BlockSpec((tm, d), lambda i, c: (i, 0)),
        out_shape=jax.ShapeDtypeStruct((t, d), F32),
        scratch_shapes=[pltpu.VMEM((tm + 2 * hl, d), BF16), pltpu.VMEM((tm, d), F32)],
        name="conv_ffn",
        compiler_params=_cparams(("parallel", "arbitrary")),
    )(x2, x2, x2, g, wup, wup, convw, convw, wdown, pg)


def _rope_tables(positions, rot_dim, period, base):
    half = rot_dim // 2
    inv = ROPE_THETA ** (-jnp.arange(0, rot_dim, 2, dtype=F32) / rot_dim)
    ang = positions.astype(F32).reshape(-1, 1) * inv[None, :]
    cos, sin = jnp.cos(ang), jnp.sin(ang)
    lane = jnp.arange(LANES)
    rel = lane % period - base
    idx = jnp.clip(rel, 0, rot_dim - 1) % half
    first = (rel >= 0) & (rel < half)
    second = (rel >= half) & (rel < rot_dim)
    cg, sg = cos[:, idx], sin[:, idx]
    c = jnp.where((first | second)[None, :], cg, 1.0)
    sa = jnp.where(first[None, :], -sg, 0.0)
    sb = jnp.where(second[None, :], sg, 0.0)
    return c, sa, sb


def _layer_weights(l, w_in, mla_w_uq, mla_w_ukv, w_branch_mla):
    d = D_MODEL
    wi = w_in[l]
    o1 = GATE_COLS
    o2 = o1 + MLA_COLS
    o3 = o2 + DIL_COLS
    w_gate = wi[:, :o1].astype(BF16)
    wm = wi[:, o1:o2]
    z = lambda n: jnp.zeros((d, n), F32)
    w_mla_in = jnp.concatenate(
        [wm[:, :MLA_Q_RANK + MLA_KV_RANK], z(MLA_NOPE), wm[:, MLA_Q_RANK + MLA_KV_RANK:], z(LANES - MLA_NOPE - MLA_ROPE)],
        axis=1).astype(BF16)
    qscale = LOG2E * HEAD_DIM ** -0.5
    wd = wi[:, o2:o3].reshape(d, DIL_GROUPS, 3, DIL_OUT)
    wd = wd * jnp.array([qscale, 1.0, 1.0], F32)[None, None, :, None]
    w_dil = wd.reshape(d, DIL_COLS).astype(BF16)
    wf = wi[:, o3:].reshape(d, 3, DIFF_OUT)
    wf = wf * jnp.array([qscale, 1.0, 1.0], F32)[None, :, None]
    w_diff = wf.reshape(d, DIFF_COLS).astype(BF16)
    qk = MLA_NOPE + MLA_ROPE
    wq = mla_w_uq[l].reshape(MLA_Q_RANK, MLA_HEADS, qk) * (LOG2E * qk ** -0.5)
    wq = jnp.pad(wq, ((0, 0), (0, 0), (0, LANES - qk))).reshape(MLA_Q_RANK, MLA_HEADS * LANES).astype(BF16)
    wkv = mla_w_ukv[l].reshape(MLA_KV_RANK, MLA_HEADS, MLA_NOPE + MLA_V)
    pad_head = lambda w_: jnp.pad(w_, ((0, 0), (0, 0), (0, LANES - w_.shape[2]))).reshape(
        MLA_KV_RANK, MLA_HEADS * LANES).astype(BF16)
    wk = pad_head(wkv[:, :, :MLA_NOPE])
    wv = pad_head(wkv[:, :, MLA_NOPE:])
    wb = w_branch_mla[l].reshape(MLA_HEADS, MLA_V, d)
    wb = jnp.pad(wb, ((0, 0), (0, LANES - MLA_V), (0, 0))).reshape(MLA_HEADS * LANES, d).astype(BF16)
    return w_gate, w_mla_in, w_dil, w_diff, wq, wk, wv, wb


def _lambda_init(layer):
    return 0.8 - 0.6 * math.exp(-0.3 * layer)


def kernel(x, positions, attn_pre_norm, w_in, mla_q_norm, mla_w_uq, mla_kv_norm, mla_w_ukv, diff_lambda,
           diff_subln, w_branch_mla, w_branch_dil, w_branch_diff, w_out, attn_post_norm, ffn_pre_norm,
           w_up, ffn_conv, w_down, ffn_post_norm):
    b, s, d = x.shape
    t = b * s
    depth = w_in.shape[0]
    tab_p = _rope_tables(positions, PARTIAL_ROT_DIM, HEAD_DIM, 0)
    tab_m = _rope_tables(positions, MLA_ROPE, LANES, MLA_NOPE)
    x2 = x.reshape(t, d)
    row = lambda a: a.reshape(1, -1)
    for l in range(depth):
        w_gate, w_mla_in, w_dil, w_diff, wq, wk, wv, wbm = _layer_weights(l, w_in, mla_w_uq, mla_w_ukv, w_branch_mla)
        g_pre = row(attn_pre_norm[l])
        gates = _inproj(x2, g_pre, w_gate, None, tn=GATE_COLS // 2, rope_chunks=0)
        gw = DIL_COLS // DIL_GROUPS
        p_dil = [_inproj(x2, g_pre, w_dil[:, gi * gw:(gi + 1) * gw], tab_p, tn=gw, rope_chunks=2 * DIL_OUT // LANES, dil=dl)
                 for gi, (_, dl) in enumerate(DIL_CONFIGS)]
        p_diff = _inproj(x2, g_pre, w_diff, tab_p, tn=DIFF_COLS, rope_chunks=2 * DIFF_OUT // LANES)
        q_m, k_m, v_m = _mla_prep(x2, g_pre, w_mla_in, row(mla_q_norm[l]), row(mla_kv_norm[l]), wq, wk, wv, tab_m)
        hw = MLA_HEADS * LANES
        o_mla = _flash_mla(q_m.reshape(b, s, hw), k_m.reshape(b, s, hw), v_m.reshape(b, s, hw)).reshape(t, hw)
        o_diff = _flash_diff(p_diff.reshape(b, s, DIFF_COLS), diff_lambda[l], row(diff_subln[l]),
                             _lambda_init(l)).reshape(t, DIFF_OUT)
        dil = [_dilated(p_dil[gi], dl, b, s) for gi, (_, dl) in enumerate(DIL_CONFIGS)]
        x2 = _merge(x2, gates, o_mla, [o for o, _ in dil], [ls for _, ls in dil], o_diff,
                    wbm, w_branch_dil[l].astype(BF16), w_branch_diff[l].astype(BF16), w_out[l].astype(BF16),
                    row(attn_post_norm[l]))
        x2 = _ffn(x2, row(ffn_pre_norm[l]), w_up[l].astype(BF16), ffn_conv[l], w_down[l].astype(BF16),
                  row(ffn_post_norm[l]), s)
    return x2.reshape(b, s, d)
```

```python
import functools
import math

import jax
import jax.numpy as jnp
from jax import lax
from jax.experimental import pallas as pl
from jax.experimental.pallas import tpu as pltpu

F32 = jnp.float32
BF16 = jnp.bfloat16

D_MODEL = 1024
ROPE_THETA = 500000.0
NORM_EPS = 1e-6

MLA_HEADS = 8
MLA_NOPE = 64
MLA_ROPE = 32
MLA_V = 64
MLA_Q_RANK = 384
MLA_KV_RANK = 256

HEAD_DIM = 64
PARTIAL_ROT_DIM = HEAD_DIM // 4
DIL_CONFIGS = ((128, 1), (512, 4), (2048, 16))
DIL_GROUPS = len(DIL_CONFIGS)
DIL_HEADS = 6
DIFF_HEADS = 4
D_FF = 2816
CONV_WIDTH = 3

GATE_COLS = 3 * D_MODEL
MLA_COLS = MLA_Q_RANK + MLA_KV_RANK + MLA_ROPE
DIL_COLS = DIL_GROUPS * 3 * DIL_HEADS * HEAD_DIM
DIFF_COLS = 3 * DIFF_HEADS * 2 * HEAD_DIM
DIL_OUT = DIL_HEADS * HEAD_DIM
DIFF_OUT = DIFF_HEADS * 2 * HEAD_DIM

LANES = 128
LOG2E = 1.4426950408889634
NEG_BIG = -1e30
VMEM_LIMIT = 56 * 1024 * 1024

TM_NORM = 1024
TM_PROJ = 512
TM_GATES = 1024
TM_MERGE = 512
TM_FFN = 512
TF_FFN = 256
FFN_HALO = 16
TQ_MLA = 1024
TQ_DIFF = 512
TK_MLA = 1024
TK_DIFF = 512
TJ_DIL = 512
DIL_RADIUS = 64
DIL_QSUB = 128
DIL_KWIN = DIL_QSUB + 2 * DIL_RADIUS


def _cparams(sem):
    return pltpu.CompilerParams(dimension_semantics=sem, vmem_limit_bytes=VMEM_LIMIT)


def _rms(x, g):
    ms = jnp.mean(x * x, axis=-1, keepdims=True)
    return x * lax.rsqrt(ms + NORM_EPS) * g


def _rope128(z, c, sa, sb, shift):
    return z * c + pltpu.roll(z, LANES - shift, 1) * sa + pltpu.roll(z, shift, 1) * sb


def _resident(a):
    return pl.BlockSpec(a.shape, lambda *_: (0,) * a.ndim, pipeline_mode=pl.Buffered(1))


def _norm_kernel(x_ref, g_ref, o_ref):
    o_ref[...] = _rms(x_ref[...], g_ref[...]).astype(o_ref.dtype)


def _norm_cast(x2, g):
    t, d = x2.shape
    tm = TM_NORM
    return pl.pallas_call(
        _norm_kernel,
        grid=(t // tm,),
        in_specs=[pl.BlockSpec((tm, d), lambda i: (i, 0)), _resident(g)],
        out_specs=pl.BlockSpec((tm, d), lambda i: (i, 0)),
        out_shape=jax.ShapeDtypeStruct((t, d), BF16),
        name="pre_norm",
        compiler_params=_cparams(("parallel",)),
    )(x2, g)


def _inproj_kernel(h_ref, w_ref, *rest, rope_chunks, n_chunks, dil):
    rest = list(rest)
    scr_ref = rest.pop() if dil > 1 else None
    o_ref = rest.pop()
    y = jnp.dot(h_ref[...], w_ref[...], preferred_element_type=F32)
    for c in range(n_chunks):
        yc = y[:, c * LANES:(c + 1) * LANES]
        if c < rope_chunks:
            c_ref, sa_ref, sb_ref = rest
            yc = _rope128(yc, c_ref[...], sa_ref[...], sb_ref[...], PARTIAL_ROT_DIM // 2)
        if dil > 1:
            scr_ref[c] = yc
        else:
            o_ref[:, c * LANES:(c + 1) * LANES] = yc.astype(o_ref.dtype)
    if dil > 1:
        rows = y.shape[0] // dil
        for r in range(dil):
            for c in range(n_chunks):
                col = (r * n_chunks + c) * LANES
                o_ref[:, col:col + LANES] = scr_ref[c, pl.ds(r, rows, stride=dil), :].astype(o_ref.dtype)


def _inproj(h, w, tables, *, tn, rope_chunks, dil=1, tm=TM_PROJ):
    t, d = h.shape
    n = w.shape[1]
    assert dil == 1 or n == tn
    in_specs = [
        pl.BlockSpec((tm, d), lambda j, i: (i, 0)),
        pl.BlockSpec((d, tn), lambda j, i: (0, j)),
    ]
    args = [h, w]
    if rope_chunks:
        in_specs += [pl.BlockSpec((tm, LANES), lambda j, i: (i, 0))] * 3
        args += list(tables)
    if dil > 1:
        out_spec = pl.BlockSpec((tm // dil, dil * tn), lambda j, i: (i, 0))
        out_shape = jax.ShapeDtypeStruct((t // dil, dil * tn), BF16)
        scratch = [pltpu.VMEM((tn // LANES, tm, LANES), F32)]
    else:
        out_spec = pl.BlockSpec((tm, tn), lambda j, i: (i, j))
        out_shape = jax.ShapeDtypeStruct((t, n), BF16)
        scratch = []
    return pl.pallas_call(
        functools.partial(_inproj_kernel, rope_chunks=rope_chunks, n_chunks=tn // LANES, dil=dil),
        grid=(n // tn, t // tm),
        in_specs=in_specs,
        out_specs=out_spec,
        out_shape=out_shape,
        scratch_shapes=scratch,
        name=f"inproj_rope_d{dil}" if rope_chunks else "inproj_gates",
        compiler_params=_cparams(("parallel", "parallel")),
    )(*args)


def _mla_prep_kernel(h_ref, win_ref, gq_ref, gkv_ref, wuq_ref, wuk_ref, wuv_ref,
                     c_ref, sa_ref, sb_ref, q_ref, k_ref, v_ref):
    y = jnp.dot(h_ref[...], win_ref[...], preferred_element_type=F32)
    cq = y[:, :MLA_Q_RANK]
    ckv = y[:, MLA_Q_RANK:MLA_Q_RANK + MLA_KV_RANK]
    kr = y[:, MLA_Q_RANK + MLA_KV_RANK:]
    qn = _rms(cq, gq_ref[...]).astype(BF16)
    kvn = _rms(ckv, gkv_ref[...]).astype(BF16)
    q = jnp.dot(qn, wuq_ref[...], preferred_element_type=F32)
    kn = jnp.dot(kvn, wuk_ref[...], preferred_element_type=F32)
    vv = jnp.dot(kvn, wuv_ref[...], preferred_element_type=F32)
    c, sa, sb = c_ref[...], sa_ref[...], sb_ref[...]
    krr = _rope128(kr, c, sa, sb, MLA_ROPE // 2)
    lane = lax.broadcasted_iota(jnp.int32, krr.shape, 1)
    for hd in range(MLA_HEADS):
        sl = slice(hd * LANES, (hd + 1) * LANES)
        q_ref[:, sl] = _rope128(q[:, sl], c, sa, sb, MLA_ROPE // 2).astype(BF16)
        k_ref[:, sl] = jnp.where(lane < MLA_NOPE, kn[:, sl], krr).astype(BF16)
        v_ref[:, sl] = jnp.where(lane < MLA_V, vv[:, sl], 1.0).astype(BF16)


def _mla_prep(h, win, gq, gkv, wuq, wuk, wuv, tables):
    t, d = h.shape
    tm = TM_PROJ
    hw = MLA_HEADS * LANES
    row = lambda w_: pl.BlockSpec((tm, w_), lambda i: (i, 0))
    out = jax.ShapeDtypeStruct((t, hw), BF16)
    wts = [win, gq, gkv, wuq, wuk, wuv]
    return pl.pallas_call(
        _mla_prep_kernel,
        grid=(t // tm,),
        in_specs=[row(d)] + [_resident(a) for a in wts] + [row(LANES)] * 3,
        out_specs=[row(hw)] * 3,
        out_shape=[out] * 3,
        name="mla_prep",
        compiler_params=_cparams(("parallel",)),
    )(h, *wts, *tables)


def _flash_sweep(q, k_at, v_at, m_ref, acc_ref, *, seq, tk):
    nv = acc_ref.shape[1]
    m_ref[...] = jnp.full(m_ref.shape, -jnp.inf, F32)
    acc_ref[...] = jnp.zeros(acc_ref.shape, F32)
    for c in range(seq // tk):
        sl = slice(c * tk, (c + 1) * tk)
        s = lax.dot_general(q, k_at(sl), (((1,), (1,)), ((), ())), preferred_element_type=F32)
        m_prev = m_ref[...]
        m_new = jnp.maximum(m_prev, jnp.max(s, axis=1, keepdims=True))
        alpha = jnp.exp2(m_prev - m_new)
        p = jnp.exp2((s - jnp.tile(m_new, (1, tk // LANES))).astype(BF16))
        acc_ref[...] = jnp.tile(alpha, (1, nv // LANES)) * acc_ref[...] + jnp.dot(
            p, v_at(sl), preferred_element_type=F32)
        m_ref[...] = m_new


def _flash_mla_kernel(q_ref, k_ref, v_ref, o_ref, m_ref, acc_ref, *, tk):
    _flash_sweep(q_ref[0], lambda sl: k_ref[0, sl, :], lambda sl: v_ref[0, sl, :], m_ref, acc_ref,
                 seq=k_ref.shape[1], tk=tk)
    acc = acc_ref[...]
    lane = lax.broadcasted_iota(jnp.int32, acc.shape, 1)
    o_ref[0] = jnp.where(lane < MLA_V, acc / pltpu.roll(acc, MLA_V, 1), 0.0).astype(o_ref.dtype)


def _flash_mla(q, k, v):
    b, s, hw = q.shape
    nh = hw // LANES
    tq, tk = min(TQ_MLA, s), min(TK_MLA, s)
    qspec = pl.BlockSpec((1, tq, LANES), lambda bi, h, i: (bi, i, h))
    kspec = pl.BlockSpec((1, s, LANES), lambda bi, h, i: (bi, 0, h))
    return pl.pallas_call(
        functools.partial(_flash_mla_kernel, tk=tk),
        grid=(b, nh, s // tq),
        in_specs=[qspec, kspec, kspec],
        out_specs=qspec,
        out_shape=jax.ShapeDtypeStruct((b, s, hw), BF16),
        scratch_shapes=[pltpu.VMEM((tq, LANES), F32)] * 2,
        name="flash_mla",
        compiler_params=_cparams(("parallel", "parallel", "arbitrary")),
    )(q, k, v)


def _flash_diff_kernel(q_ref, k_ref, v_ref, lam_ref, subln_ref, o_ref, v1_ref, m_ref, acc_ref, *, tk, lam_init):
    @pl.when(pl.program_id(2) == 0)
    def _():
        v1_ref[:, :LANES] = v_ref[0]
        v1_ref[:, LANES:] = jnp.ones((v1_ref.shape[0], LANES), BF16)

    q = q_ref[0]
    tq = q.shape[0]
    lane = lax.broadcasted_iota(jnp.int32, q.shape, 1)
    zero = jnp.zeros_like(q)
    q2 = jnp.concatenate([jnp.where(lane < HEAD_DIM, q, zero), jnp.where(lane >= HEAD_DIM, q, zero)], axis=0)
    _flash_sweep(q2, lambda sl: k_ref[0, sl, :], lambda sl: v1_ref[sl, :], m_ref, acc_ref,
                 seq=k_ref.shape[1], tk=tk)
    acc = acc_ref[...]
    o12 = acc[:, :LANES] / acc[:, LANES:]
    lv = lam_ref[...]
    lam = (jnp.exp(jnp.sum(lv[0:1] * lv[1:2], axis=1, keepdims=True))
           - jnp.exp(jnp.sum(lv[2:3] * lv[3:4], axis=1, keepdims=True)) + lam_init)
    o = o12[:tq] - lam * o12[tq:]
    o_ref[0] = (_rms(o, subln_ref[...]) * (1.0 - lam_init)).astype(o_ref.dtype)


def _flash_diff(proj, lam_vecs, subln, lam_init):
    b, s, _ = proj.shape
    nh = DIFF_HEADS
    tq, tk = min(TQ_DIFF, s), min(TK_DIFF, s)
    return pl.pallas_call(
        functools.partial(_flash_diff_kernel, tk=tk, lam_init=lam_init),
        grid=(b, nh, s // tq),
        in_specs=[
            pl.BlockSpec((1, tq, LANES), lambda bi, h, i: (bi, i, h)),
            pl.BlockSpec((1, s, LANES), lambda bi, h, i: (bi, 0, nh + h)),
            pl.BlockSpec((1, s, LANES), lambda bi, h, i: (bi, 0, 2 * nh + h)),
            pl.BlockSpec(lam_vecs.shape, lambda bi, h, i: (0, 0)),
            pl.BlockSpec(subln.shape, lambda bi, h, i: (0, 0)),
        ],
        out_specs=pl.BlockSpec((1, tq, LANES), lambda bi, h, i: (bi, i, h)),
        out_shape=jax.ShapeDtypeStruct((b, s, DIFF_OUT), BF16),
        scratch_shapes=[pltpu.VMEM((s, 2 * LANES), BF16), pltpu.VMEM((2 * tq, LANES), F32),
                        pltpu.VMEM((2 * tq, 2 * LANES), F32)],
        name="flash_diff",
        compiler_params=_cparams(("parallel", "parallel", "arbitrary")),
    )(proj, proj, proj, lam_vecs, subln)


def _dil_kernel(q_ref, kp_ref, kc_ref, kn_ref, vp_ref, vc_ref, vn_ref, o_ref, lse_ref, *, tj, lsub):
    j0 = pl.program_id(2) * tj
    r = DIL_RADIUS
    qs = DIL_QSUB
    row = lax.broadcasted_iota(jnp.int32, (2 * qs, DIL_KWIN), 0) & (qs - 1)
    col = lax.broadcasted_iota(jnp.int32, (2 * qs, DIL_KWIN), 1)
    band = jnp.abs(col - r - row) <= r
    biases = []
    for i in range(tj // qs):
        kpos = j0 + (i * qs - r) + col
        biases.append(jnp.where(band & (kpos >= 0) & (kpos < lsub), 0.0, NEG_BIG))
    low = lax.broadcasted_iota(jnp.int32, (qs, LANES), 1) < HEAD_DIM
    ones = jnp.ones((tj + 2 * r, LANES), BF16)
    for c in range(DIL_OUT // LANES):
        sl = slice(c * LANES, (c + 1) * LANES)
        kcat = jnp.concatenate([kp_ref[0, tj - r:tj, sl], kc_ref[0, :, sl], kn_ref[0, 0:r, sl]], axis=0)
        vcat = jnp.concatenate([vp_ref[0, tj - r:tj, sl], vc_ref[0, :, sl], vn_ref[0, 0:r, sl]], axis=0)
        vcat = jnp.concatenate([vcat, ones], axis=1)
        for i in range(tj // qs):
            rows = slice(i * qs, (i + 1) * qs)
            q = q_ref[0, rows, sl]
            zero = jnp.zeros_like(q)
            q2 = jnp.concatenate([jnp.where(low, q, zero), jnp.where(low, zero, q)], axis=0)
            kw = kcat[i * qs:i * qs + DIL_KWIN]
            vw = vcat[i * qs:i * qs + DIL_KWIN]
            s = lax.dot_general(q2, kw, (((1,), (1,)), ((), ())), preferred_element_type=F32) + biases[i]
            m = jnp.max(s, axis=1, keepdims=True)
            p = jnp.exp2((s - m).astype(BF16))
            pv = jnp.dot(p, vw, preferred_element_type=F32)
            den = pv[:, LANES:]
            o2 = pv[:, :LANES] / den
            lse2 = m + jnp.log2(den)
            o_ref[0, rows, sl] = jnp.where(low, o2[:qs], o2[qs:])
            lse_ref[0, rows, sl] = jnp.where(low, lse2[:qs], lse2[qs:])


def _dilated(proj, dil, b, s):
    w = DIL_OUT
    lsub = s // dil
    tj = min(TJ_DIL, lsub)
    nblk = lsub // tj
    view = proj.reshape(b, lsub, dil * 3 * w)

    def spec(t, shift):
        def imap(bi, ri, ji):
            jj = jnp.clip(ji + shift, 0, nblk - 1)
            return (bi, jj, ri * 3 + t)
        return pl.BlockSpec((1, tj, w), imap)

    ospec = pl.BlockSpec((1, tj, w), lambda bi, ri, ji: (bi, ji, ri))
    oshape = jax.ShapeDtypeStruct((b, lsub, dil * w), F32)
    o, lse = pl.pallas_call(
        functools.partial(_dil_kernel, tj=tj, lsub=lsub),
        grid=(b, dil, nblk),
        in_specs=[spec(0, 0), spec(1, -1), spec(1, 0), spec(1, 1), spec(2, -1), spec(2, 0), spec(2, 1)],
        out_specs=[ospec, ospec],
        out_shape=[oshape, oshape],
        name=f"dilated_d{dil}",
        compiler_params=_cparams(("parallel", "parallel", "parallel")),
    )(view, view, view, view, view, view, view)
    return o.reshape(b * lsub, dil * w), lse.reshape(b * lsub, dil * w)


def _merge_kernel(x_ref, gate_ref, omla_ref, od0, od1, od2, ls0, ls1, ls2, odiff_ref,
                  wbm_ref, wbd_ref, wbf_ref, wo_ref, pg_ref, gn_ref, o_ref, hn_ref, scr_ref):
    tm = x_ref.shape[0]
    nch = DIL_OUT // LANES

    def natural(ref, dil):
        if dil == 1:
            return ref[...]
        rows = tm // dil
        for r in range(dil):
            for c in range(nch):
                col = (r * nch + c) * LANES
                scr_ref[c, pl.ds(r, rows, stride=dil), :] = ref[:, col:col + LANES]
        return jnp.concatenate([scr_ref[c] for c in range(nch)], axis=1)

    dils = [dl for _, dl in DIL_CONFIGS]
    l0, l1, l2 = (natural(r_, dl) for r_, dl in zip((ls0, ls1, ls2), dils))
    mx = jnp.maximum(jnp.maximum(l0, l1), l2)
    e0, e1, e2 = jnp.exp2(l0 - mx), jnp.exp2(l1 - mx), jnp.exp2(l2 - mx)
    den = e0 + e1 + e2
    num = e0 * natural(od0, dils[0])
    num = num + e1 * natural(od1, dils[1])
    num = num + e2 * natural(od2, dils[2])
    odil = (num / den).astype(BF16)
    d = D_MODEL
    y = jax.nn.sigmoid(gate_ref[:, 0:d].astype(F32)) * jnp.dot(omla_ref[...], wbm_ref[...], preferred_element_type=F32)
    y = y + jax.nn.sigmoid(gate_ref[:, d:2 * d].astype(F32)) * jnp.dot(odil, wbd_ref[...], preferred_element_type=F32)
    y = y + jax.nn.sigmoid(gate_ref[:, 2 * d:3 * d].astype(F32)) * jnp.dot(odiff_ref[...], wbf_ref[...], preferred_element_type=F32)
    z = jnp.dot(y.astype(BF16), wo_ref[...], preferred_element_type=F32)
    xn = x_ref[...] + _rms(z, pg_ref[...])
    o_ref[...] = xn
    hn_ref[...] = _rms(xn, gn_ref[...]).astype(hn_ref.dtype)


def _merge(x2, gates, omla, odil, lses, odiff, wbm, wbd, wbf, wo, pg, g_next):
    t, d = x2.shape
    tm = TM_MERGE
    row = lambda a: pl.BlockSpec((tm * a.shape[0] // t, a.shape[1]), lambda i: (i, 0))
    acts = [x2, gates, omla, *odil, *lses, odiff]
    wts = [wbm, wbd, wbf, wo, pg, g_next]
    return pl.pallas_call(
        _merge_kernel,
        grid=(t // tm,),
        in_specs=[row(a) for a in acts] + [_resident(a) for a in wts],
        out_specs=[pl.BlockSpec((tm, d), lambda i: (i, 0))] * 2,
        out_shape=[jax.ShapeDtypeStruct((t, d), F32), jax.ShapeDtypeStruct((t, d), BF16)],
        scratch_shapes=[pltpu.VMEM((DIL_OUT // LANES, tm, LANES), F32)],
        name="merge",
        compiler_params=_cparams(("parallel",)),
    )(*acts, *wts)


def _ffn_kernel(hp_ref, h_ref, hn_ref, x_ref, wup_ref, cw_ref, wd_ref, pg_ref, *rest, tm, seq, has_next):
    if has_next:
        gn_ref, o_ref, hnext_ref, hext_ref, a_ref = rest
    else:
        o_ref, hext_ref, a_ref = rest
    i = pl.program_id(0)
    hl = FFN_HALO
    keep_p = (i * tm) % seq != 0
    keep_n = ((i + 1) * tm) % seq != 0
    hext_ref[0:hl, :] = jnp.where(keep_p, hp_ref[...], jnp.zeros_like(hp_ref))
    hext_ref[hl:hl + tm, :] = h_ref[...]
    hext_ref[hl + tm:2 * hl + tm, :] = jnp.where(keep_n, hn_ref[...], jnp.zeros_like(hn_ref))
    hx = hext_ref[...]
    n_ext = tm + 2 * hl

    def conv(u, cw):
        up = pltpu.roll(u, 1, 0)
        dn = pltpu.roll(u, n_ext - 1, 0)
        z = cw[0:1] * up + cw[1:2] * u + cw[2:3] * dn
        return z[hl:hl + tm]

    for c in range(D_FF // TF_FFN):
        sg = slice(c * TF_FFN, (c + 1) * TF_FFN)
        sv = slice(D_FF + c * TF_FFN, D_FF + (c + 1) * TF_FFN)
        ug = conv(jnp.dot(hx, wup_ref[:, sg], preferred_element_type=F32), cw_ref[:, sg])
        uv = conv(jnp.dot(hx, wup_ref[:, sv], preferred_element_type=F32), cw_ref[:, sv])
        a_ref[:, sg] = (jax.nn.gelu(ug, approximate=True) * uv).astype(BF16)
    y = jnp.dot(a_ref[...], wd_ref[...], preferred_element_type=F32)
    xn = x_ref[...] + _rms(y, pg_ref[...])
    o_ref[...] = xn
    if has_next:
        hnext_ref[...] = _rms(xn, gn_ref[...]).astype(hnext_ref.dtype)


def _ffn(x2, h, wup, convw, wdown, pg, g_next, seq):
    t, d = x2.shape
    tm = min(TM_FFN, seq)
    hl = FFN_HALO
    rb = tm // hl
    nhb = t // hl
    has_next = g_next is not None
    row = pl.BlockSpec((tm, d), lambda i: (i, 0))
    wts = [wup, convw, wdown, pg] + ([g_next] if has_next else [])
    outs = [jax.ShapeDtypeStruct((t, d), F32)] + ([jax.ShapeDtypeStruct((t, d), BF16)] if has_next else [])
    res = pl.pallas_call(
        functools.partial(_ffn_kernel, tm=tm, seq=seq, has_next=has_next),
        grid=(t // tm,),
        in_specs=[
            pl.BlockSpec((hl, d), lambda i: (jnp.maximum(i * rb - 1, 0), 0)),
            row,
            pl.BlockSpec((hl, d), lambda i: (jnp.minimum((i + 1) * rb, nhb - 1), 0)),
            row,
        ] + [_resident(a) for a in wts],
        out_specs=[row] * len(outs),
        out_shape=outs,
        scratch_shapes=[pltpu.VMEM((tm + 2 * hl, d), BF16), pltpu.VMEM((tm, D_FF), BF16)],
        name="conv_ffn",
        compiler_params=_cparams(("parallel",)),
    )(h, h, h, x2, *wts)
    return (res[0], res[1]) if has_next else (res[0], None)


def _rope_tables(positions, rot_dim, period, base):
    half = rot_dim // 2
    inv = ROPE_THETA ** (-jnp.arange(0, rot_dim, 2, dtype=F32) / rot_dim)
    ang = positions.astype(F32).reshape(-1, 1) * inv[None, :]
    cos, sin = jnp.cos(ang), jnp.sin(ang)
    lane = jnp.arange(LANES)
    rel = lane % period - base
    idx = jnp.clip(rel, 0, rot_dim - 1) % half
    first = (rel >= 0) & (rel < half)
    second = (rel >= half) & (rel < rot_dim)
    cg, sg = cos[:, idx], sin[:, idx]
    c = jnp.where((first | second)[None, :], cg, 1.0)
    sa = jnp.where(first[None, :], -sg, 0.0)
    sb = jnp.where(second[None, :], sg, 0.0)
    return c, sa, sb


def _layer_weights(l, w_in, mla_w_uq, mla_w_ukv, w_branch_mla):
    d = D_MODEL
    wi = w_in[l]
    o1 = GATE_COLS
    o2 = o1 + MLA_COLS
    o3 = o2 + DIL_COLS
    w_gate = wi[:, :o1].astype(BF16)
    wm = wi[:, o1:o2]
    z = lambda n: jnp.zeros((d, n), F32)
    w_mla_in = jnp.concatenate(
        [wm[:, :MLA_Q_RANK + MLA_KV_RANK], z(MLA_NOPE), wm[:, MLA_Q_RANK + MLA_KV_RANK:], z(LANES - MLA_NOPE - MLA_ROPE)],
        axis=1).astype(BF16)
    qscale = LOG2E * HEAD_DIM ** -0.5
    wd = wi[:, o2:o3].reshape(d, DIL_GROUPS, 3, DIL_OUT)
    wd = wd * jnp.array([qscale, 1.0, 1.0], F32)[None, None, :, None]
    w_dil = wd.reshape(d, DIL_COLS).astype(BF16)
    wf = wi[:, o3:].reshape(d, 3, DIFF_OUT)
    wf = wf * jnp.array([qscale, 1.0, 1.0], F32)[None, :, None]
    w_diff = wf.reshape(d, DIFF_COLS).astype(BF16)
    qk = MLA_NOPE + MLA_ROPE
    wq = mla_w_uq[l].reshape(MLA_Q_RANK, MLA_HEADS, qk) * (LOG2E * qk ** -0.5)
    wq = jnp.pad(wq, ((0, 0), (0, 0), (0, LANES - qk))).reshape(MLA_Q_RANK, MLA_HEADS * LANES).astype(BF16)
    wkv = mla_w_ukv[l].reshape(MLA_KV_RANK, MLA_HEADS, MLA_NOPE + MLA_V)
    pad_head = lambda w_: jnp.pad(w_, ((0, 0), (0, 0), (0, LANES - w_.shape[2]))).reshape(
        MLA_KV_RANK, MLA_HEADS * LANES).astype(BF16)
    wk = pad_head(wkv[:, :, :MLA_NOPE])
    wv = pad_head(wkv[:, :, MLA_NOPE:])
    wb = w_branch_mla[l].reshape(MLA_HEADS, MLA_V, d)
    wb = jnp.pad(wb, ((0, 0), (0, LANES - MLA_V), (0, 0))).reshape(MLA_HEADS * LANES, d).astype(BF16)
    return w_gate, w_mla_in, w_dil, w_diff, wq, wk, wv, wb


def _lambda_init(layer):
    return 0.8 - 0.6 * math.exp(-0.3 * layer)


def kernel(x, positions, attn_pre_norm, w_in, mla_q_norm, mla_w_uq, mla_kv_norm, mla_w_ukv, diff_lambda,
           diff_subln, w_branch_mla, w_branch_dil, w_branch_diff, w_out, attn_post_norm, ffn_pre_norm,
           w_up, ffn_conv, w_down, ffn_post_norm):
    b, s, d = x.shape
    t = b * s
    depth = w_in.shape[0]
    tab_p = _rope_tables(positions, PARTIAL_ROT_DIM, HEAD_DIM, 0)
    tab_m = _rope_tables(positions, MLA_ROPE, LANES, MLA_NOPE)
    x2 = x.reshape(t, d)
    row = lambda a: a.reshape(1, -1)
    h = _norm_cast(x2, row(attn_pre_norm[0]))
    for l in range(depth):
        w_gate, w_mla_in, w_dil, w_diff, wq, wk, wv, wbm = _layer_weights(l, w_in, mla_w_uq, mla_w_ukv, w_branch_mla)
        gates = _inproj(h, w_gate, None, tn=GATE_COLS // 2, rope_chunks=0, tm=TM_GATES)
        gw = DIL_COLS // DIL_GROUPS
        p_dil = [_inproj(h, w_dil[:, gi * gw:(gi + 1) * gw], tab_p, tn=gw, rope_chunks=2 * DIL_OUT // LANES, dil=dl)
                 for gi, (_, dl) in enumerate(DIL_CONFIGS)]
        p_diff = _inproj(h, w_diff, tab_p, tn=DIFF_COLS, rope_chunks=2 * DIFF_OUT // LANES)
        q_m, k_m, v_m = _mla_prep(h, w_mla_in, row(mla_q_norm[l]), row(mla_kv_norm[l]), wq, wk, wv, tab_m)
        hw = MLA_HEADS * LANES
        o_mla = _flash_mla(q_m.reshape(b, s, hw), k_m.reshape(b, s, hw), v_m.reshape(b, s, hw)).reshape(t, hw)
        o_diff = _flash_diff(p_diff.reshape(b, s, DIFF_COLS), diff_lambda[l], row(diff_subln[l]),
                             _lambda_init(l)).reshape(t, DIFF_OUT)
        dil = [_dilated(p_dil[gi], dl, b, s) for gi, (_, dl) in enumerate(DIL_CONFIGS)]
        x2, h_ffn = _merge(x2, gates, o_mla, [o for o, _ in dil], [ls for _, ls in dil], o_diff,
                           wbm, w_branch_dil[l].astype(BF16), w_branch_diff[l].astype(BF16), w_out[l].astype(BF16),
                           row(attn_post_norm[l]), row(ffn_pre_norm[l]))
        g_next = row(attn_pre_norm[l + 1]) if l + 1 < depth else None
        x2, h = _ffn(x2, h_ffn, w_up[l].astype(BF16), ffn_conv[l], w_down[l].astype(BF16),
                     row(ffn_post_norm[l]), g_next, s)
    return x2.reshape(b, s, d)
```

```python
import functools
import math

import jax
import jax.numpy as jnp
from jax import lax
from jax.experimental import pallas as pl
from jax.experimental.pallas import tpu as pltpu

F32 = jnp.float32
BF16 = jnp.bfloat16

D_MODEL = 1024
ROPE_THETA = 500000.0
NORM_EPS = 1e-6

MLA_HEADS = 8
MLA_NOPE = 64
MLA_ROPE = 32
MLA_V = 64
MLA_Q_RANK = 384
MLA_KV_RANK = 256

HEAD_DIM = 64
PARTIAL_ROT_DIM = HEAD_DIM // 4
DIL_CONFIGS = ((128, 1), (512, 4), (2048, 16))
DIL_GROUPS = len(DIL_CONFIGS)
DIL_HEADS = 6
DIFF_HEADS = 4
D_FF = 2816
CONV_WIDTH = 3

GATE_COLS = 3 * D_MODEL
MLA_COLS = MLA_Q_RANK + MLA_KV_RANK + MLA_ROPE
DIL_COLS = DIL_GROUPS * 3 * DIL_HEADS * HEAD_DIM
DIFF_COLS = 3 * DIFF_HEADS * 2 * HEAD_DIM
DIL_OUT = DIL_HEADS * HEAD_DIM
DIFF_OUT = DIFF_HEADS * 2 * HEAD_DIM

LANES = 128
LOG2E = 1.4426950408889634
NEG_BIG = -1e30
VMEM_LIMIT = 56 * 1024 * 1024

TM_NORM = 1024
TM_PROJ = 512
TM_GATES = 1024
TM_MERGE = 512
MERGE_SPLIT = 2
TM_FFN = 512
TF_FFN = 256
FFN_HALO = 16
TQ_MLA = 1024
TQ_DIFF = 512
TK_MLA = 1024
TK_DIFF = 512
TJ_DIL = 512
DIL_RADIUS = 64
DIL_QSUB = 128
DIL_KWIN = DIL_QSUB + 2 * DIL_RADIUS


def _cparams(sem):
    return pltpu.CompilerParams(dimension_semantics=sem, vmem_limit_bytes=VMEM_LIMIT)


def _rms(x, g):
    ms = jnp.mean(x * x, axis=-1, keepdims=True)
    return x * lax.rsqrt(ms + NORM_EPS) * g


def _rope128(z, c, sa, sb, shift):
    return z * c + pltpu.roll(z, LANES - shift, 1) * sa + pltpu.roll(z, shift, 1) * sb


def _resident(a):
    return pl.BlockSpec(a.shape, lambda *_: (0,) * a.ndim, pipeline_mode=pl.Buffered(1))


def _norm_kernel(x_ref, g_ref, o_ref):
    o_ref[...] = _rms(x_ref[...], g_ref[...]).astype(o_ref.dtype)


def _norm_cast(x2, g):
    t, d = x2.shape
    tm = TM_NORM
    return pl.pallas_call(
        _norm_kernel,
        grid=(t // tm,),
        in_specs=[pl.BlockSpec((tm, d), lambda i: (i, 0)), _resident(g)],
        out_specs=pl.BlockSpec((tm, d), lambda i: (i, 0)),
        out_shape=jax.ShapeDtypeStruct((t, d), BF16),
        name="pre_norm",
        compiler_params=_cparams(("parallel",)),
    )(x2, g)


def _inproj_kernel(h_ref, w_ref, *rest, rope_chunks, n_chunks, dil):
    rest = list(rest)
    scr_ref = rest.pop() if dil > 1 else None
    o_ref = rest.pop()
    y = jnp.dot(h_ref[...], w_ref[...], preferred_element_type=F32)
    for c in range(n_chunks):
        yc = y[:, c * LANES:(c + 1) * LANES]
        if c < rope_chunks:
            c_ref, sa_ref, sb_ref = rest
            yc = _rope128(yc, c_ref[...], sa_ref[...], sb_ref[...], PARTIAL_ROT_DIM // 2)
        if dil > 1:
            scr_ref[c] = yc
        else:
            o_ref[:, c * LANES:(c + 1) * LANES] = yc.astype(o_ref.dtype)
    if dil > 1:
        rows = y.shape[0] // dil
        for r in range(dil):
            for c in range(n_chunks):
                col = (r * n_chunks + c) * LANES
                o_ref[:, col:col + LANES] = scr_ref[c, pl.ds(r, rows, stride=dil), :].astype(o_ref.dtype)


def _inproj(h, w, tables, *, tn, rope_chunks, dil=1, tm=TM_PROJ):
    t, d = h.shape
    n = w.shape[1]
    assert dil == 1 or n == tn
    in_specs = [
        pl.BlockSpec((tm, d), lambda j, i: (i, 0)),
        pl.BlockSpec((d, tn), lambda j, i: (0, j)),
    ]
    args = [h, w]
    if rope_chunks:
        in_specs += [pl.BlockSpec((tm, LANES), lambda j, i: (i, 0))] * 3
        args += list(tables)
    if dil > 1:
        out_spec = pl.BlockSpec((tm // dil, dil * tn), lambda j, i: (i, 0))
        out_shape = jax.ShapeDtypeStruct((t // dil, dil * tn), BF16)
        scratch = [pltpu.VMEM((tn // LANES, tm, LANES), F32)]
    else:
        out_spec = pl.BlockSpec((tm, tn), lambda j, i: (i, j))
        out_shape = jax.ShapeDtypeStruct((t, n), BF16)
        scratch = []
    return pl.pallas_call(
        functools.partial(_inproj_kernel, rope_chunks=rope_chunks, n_chunks=tn // LANES, dil=dil),
        grid=(n // tn, t // tm),
        in_specs=in_specs,
        out_specs=out_spec,
        out_shape=out_shape,
        scratch_shapes=scratch,
        name=f"inproj_rope_d{dil}" if rope_chunks else "inproj_gates",
        compiler_params=_cparams(("parallel", "parallel")),
    )(*args)


def _mla_prep_kernel(h_ref, win_ref, gq_ref, gkv_ref, wuq_ref, wuk_ref, wuv_ref,
                     c_ref, sa_ref, sb_ref, q_ref, k_ref, v_ref):
    y = jnp.dot(h_ref[...], win_ref[...], preferred_element_type=F32)
    cq = y[:, :MLA_Q_RANK]
    ckv = y[:, MLA_Q_RANK:MLA_Q_RANK + MLA_KV_RANK]
    kr = y[:, MLA_Q_RANK + MLA_KV_RANK:]
    qn = _rms(cq, gq_ref[...]).astype(BF16)
    kvn = _rms(ckv, gkv_ref[...]).astype(BF16)
    q = jnp.dot(qn, wuq_ref[...], preferred_element_type=F32)
    kn = jnp.dot(kvn, wuk_ref[...], preferred_element_type=F32)
    vv = jnp.dot(kvn, wuv_ref[...], preferred_element_type=F32)
    c, sa, sb = c_ref[...], sa_ref[...], sb_ref[...]
    krr = _rope128(kr, c, sa, sb, MLA_ROPE // 2)
    lane = lax.broadcasted_iota(jnp.int32, krr.shape, 1)
    for hd in range(MLA_HEADS):
        sl = slice(hd * LANES, (hd + 1) * LANES)
        q_ref[:, sl] = _rope128(q[:, sl], c, sa, sb, MLA_ROPE // 2).astype(BF16)
        k_ref[:, sl] = jnp.where(lane < MLA_NOPE, kn[:, sl], krr).astype(BF16)
        v_ref[:, sl] = jnp.where(lane < MLA_V, vv[:, sl], 1.0).astype(BF16)


def _mla_prep(h, win, gq, gkv, wuq, wuk, wuv, tables):
    t, d = h.shape
    tm = TM_PROJ
    hw = MLA_HEADS * LANES
    row = lambda w_: pl.BlockSpec((tm, w_), lambda i: (i, 0))
    out = jax.ShapeDtypeStruct((t, hw), BF16)
    wts = [win, gq, gkv, wuq, wuk, wuv]
    return pl.pallas_call(
        _mla_prep_kernel,
        grid=(t // tm,),
        in_specs=[row(d)] + [_resident(a) for a in wts] + [row(LANES)] * 3,
        out_specs=[row(hw)] * 3,
        out_shape=[out] * 3,
        name="mla_prep",
        compiler_params=_cparams(("parallel",)),
    )(h, *wts, *tables)


def _flash_sweep(q, k_at, v_at, m_ref, acc_ref, *, seq, tk):
    nv = acc_ref.shape[1]
    for c in range(seq // tk):
        sl = slice(c * tk, (c + 1) * tk)
        s = lax.dot_general(q, k_at(sl), (((1,), (1,)), ((), ())), preferred_element_type=F32)
        m_cur = jnp.max(s, axis=1, keepdims=True)
        if c == 0:
            m_new = jnp.broadcast_to(m_cur, m_ref.shape)
        else:
            m_prev = m_ref[...]
            m_new = jnp.maximum(m_prev, m_cur)
            alpha = jnp.exp2(m_prev - m_new)
        p = jnp.exp2((s - jnp.tile(m_new, (1, tk // LANES))).astype(BF16))
        pv = jnp.dot(p, v_at(sl), preferred_element_type=F32)
        acc_ref[...] = pv if c == 0 else jnp.tile(alpha, (1, nv // LANES)) * acc_ref[...] + pv
        m_ref[...] = m_new


def _flash_mla_kernel(q_ref, k_ref, v_ref, o_ref, m_ref, acc_ref, *, tk):
    _flash_sweep(q_ref[0], lambda sl: k_ref[0, sl, :], lambda sl: v_ref[0, sl, :], m_ref, acc_ref,
                 seq=k_ref.shape[1], tk=tk)
    acc = acc_ref[...]
    lane = lax.broadcasted_iota(jnp.int32, acc.shape, 1)
    o_ref[0] = jnp.where(lane < MLA_V, acc / pltpu.roll(acc, MLA_V, 1), 0.0).astype(o_ref.dtype)


def _flash_mla(q, k, v):
    b, s, hw = q.shape
    nh = hw // LANES
    tq, tk = min(TQ_MLA, s), min(TK_MLA, s)
    qspec = pl.BlockSpec((1, tq, LANES), lambda bi, h, i: (bi, i, h))
    kspec = pl.BlockSpec((1, s, LANES), lambda bi, h, i: (bi, 0, h))
    return pl.pallas_call(
        functools.partial(_flash_mla_kernel, tk=tk),
        grid=(b, nh, s // tq),
        in_specs=[qspec, kspec, kspec],
        out_specs=qspec,
        out_shape=jax.ShapeDtypeStruct((b, s, hw), BF16),
        scratch_shapes=[pltpu.VMEM((tq, LANES), F32)] * 2,
        name="flash_mla",
        compiler_params=_cparams(("parallel", "parallel", "arbitrary")),
    )(q, k, v)


def _flash_diff_kernel(q_ref, k_ref, v_ref, lam_ref, subln_ref, o_ref, v1_ref, m_ref, acc_ref, *, tk, lam_init):
    @pl.when(pl.program_id(2) == 0)
    def _():
        v1_ref[:, :LANES] = v_ref[0]
        v1_ref[:, LANES:] = jnp.ones((v1_ref.shape[0], LANES), BF16)

    q = q_ref[0]
    tq = q.shape[0]
    lane = lax.broadcasted_iota(jnp.int32, q.shape, 1)
    zero = jnp.zeros_like(q)
    q2 = jnp.concatenate([jnp.where(lane < HEAD_DIM, q, zero), jnp.where(lane >= HEAD_DIM, q, zero)], axis=0)
    _flash_sweep(q2, lambda sl: k_ref[0, sl, :], lambda sl: v1_ref[sl, :], m_ref, acc_ref,
                 seq=k_ref.shape[1], tk=tk)
    acc = acc_ref[...]
    o12 = acc[:, :LANES] / acc[:, LANES:]
    lv = lam_ref[...]
    lam = (jnp.exp(jnp.sum(lv[0:1] * lv[1:2], axis=1, keepdims=True))
           - jnp.exp(jnp.sum(lv[2:3] * lv[3:4], axis=1, keepdims=True)) + lam_init)
    o = o12[:tq] - lam * o12[tq:]
    o_ref[0] = (_rms(o, subln_ref[...]) * (1.0 - lam_init)).astype(o_ref.dtype)


def _flash_diff(proj, lam_vecs, subln, lam_init):
    b, s, _ = proj.shape
    nh = DIFF_HEADS
    tq, tk = min(TQ_DIFF, s), min(TK_DIFF, s)
    return pl.pallas_call(
        functools.partial(_flash_diff_kernel, tk=tk, lam_init=lam_init),
        grid=(b, nh, s // tq),
        in_specs=[
            pl.BlockSpec((1, tq, LANES), lambda bi, h, i: (bi, i, h)),
            pl.BlockSpec((1, s, LANES), lambda bi, h, i: (bi, 0, nh + h)),
            pl.BlockSpec((1, s, LANES), lambda bi, h, i: (bi, 0, 2 * nh + h)),
            pl.BlockSpec(lam_vecs.shape, lambda bi, h, i: (0, 0)),
            pl.BlockSpec(subln.shape, lambda bi, h, i: (0, 0)),
        ],
        out_specs=pl.BlockSpec((1, tq, LANES), lambda bi, h, i: (bi, i, h)),
        out_shape=jax.ShapeDtypeStruct((b, s, DIFF_OUT), BF16),
        scratch_shapes=[pltpu.VMEM((s, 2 * LANES), BF16), pltpu.VMEM((2 * tq, LANES), F32),
                        pltpu.VMEM((2 * tq, 2 * LANES), F32)],
        name="flash_diff",
        compiler_params=_cparams(("parallel", "parallel", "arbitrary")),
    )(proj, proj, proj, lam_vecs, subln)


def _dil_kernel(q_ref, kp_ref, kc_ref, kn_ref, vp_ref, vc_ref, vn_ref, o_ref, lse_ref, *, tj, lsub):
    j0 = pl.program_id(2) * tj
    r = DIL_RADIUS
    qs = DIL_QSUB
    row = lax.broadcasted_iota(jnp.int32, (2 * qs, DIL_KWIN), 0) & (qs - 1)
    col = lax.broadcasted_iota(jnp.int32, (2 * qs, DIL_KWIN), 1)
    band = jnp.abs(col - r - row) <= r
    biases = []
    for i in range(tj // qs):
        kpos = j0 + (i * qs - r) + col
        biases.append(jnp.where(band & (kpos >= 0) & (kpos < lsub), 0.0, NEG_BIG))
    low = lax.broadcasted_iota(jnp.int32, (qs, LANES), 1) < HEAD_DIM
    ones = jnp.ones((tj + 2 * r, LANES), BF16)
    for c in range(DIL_OUT // LANES):
        sl = slice(c * LANES, (c + 1) * LANES)
        kcat = jnp.concatenate([kp_ref[0, :, sl], kc_ref[0, :, sl], kn_ref[0, :, sl]], axis=0)
        vcat = jnp.concatenate([vp_ref[0, :, sl], vc_ref[0, :, sl], vn_ref[0, :, sl]], axis=0)
        vcat = jnp.concatenate([vcat, ones], axis=1)
        for i in range(tj // qs):
            rows = slice(i * qs, (i + 1) * qs)
            q = q_ref[0, rows, sl]
            zero = jnp.zeros_like(q)
            q2 = jnp.concatenate([jnp.where(low, q, zero), jnp.where(low, zero, q)], axis=0)
            kw = kcat[i * qs:i * qs + DIL_KWIN]
            vw = vcat[i * qs:i * qs + DIL_KWIN]
            s = lax.dot_general(q2, kw, (((1,), (1,)), ((), ())), preferred_element_type=F32) + biases[i]
            m = jnp.max(s, axis=1, keepdims=True)
            p = jnp.exp2((s - m).astype(BF16))
            pv = jnp.dot(p, vw, preferred_element_type=F32)
            den = pv[:, LANES:]
            o2 = pv[:, :LANES] / den
            lse2 = m + jnp.log2(den)
            o_ref[0, rows, sl] = jnp.where(low, o2[:qs], o2[qs:]).astype(o_ref.dtype)
            lse_ref[0, rows, sl] = jnp.where(low, lse2[:qs], lse2[qs:])


def _dilated(proj, dil, b, s):
    w = DIL_OUT
    lsub = s // dil
    tj = min(TJ_DIL, lsub)
    nblk = lsub // tj
    view = proj.reshape(b, lsub, dil * 3 * w)

    hr = DIL_RADIUS
    nhalo = lsub // hr

    def spec(t, shift):
        if shift == 0:
            return pl.BlockSpec((1, tj, w), lambda bi, ri, ji: (bi, ji, ri * 3 + t))
        if shift < 0:
            return pl.BlockSpec((1, hr, w), lambda bi, ri, ji: (bi, jnp.maximum(ji * (tj // hr) - 1, 0), ri * 3 + t))
        return pl.BlockSpec((1, hr, w), lambda bi, ri, ji: (bi, jnp.minimum((ji + 1) * (tj // hr), nhalo - 1), ri * 3 + t))

    ospec = pl.BlockSpec((1, tj, w), lambda bi, ri, ji: (bi, ji, ri))
    o, lse = pl.pallas_call(
        functools.partial(_dil_kernel, tj=tj, lsub=lsub),
        grid=(b, dil, nblk),
        in_specs=[spec(0, 0), spec(1, -1), spec(1, 0), spec(1, 1), spec(2, -1), spec(2, 0), spec(2, 1)],
        out_specs=[ospec, ospec],
        out_shape=[jax.ShapeDtypeStruct((b, lsub, dil * w), BF16), jax.ShapeDtypeStruct((b, lsub, dil * w), F32)],
        name=f"dilated_d{dil}",
        compiler_params=_cparams(("parallel", "parallel", "parallel")),
    )(view, view, view, view, view, view, view)
    return o.reshape(b * lsub, dil * w), lse.reshape(b * lsub, dil * w)


def _merge_kernel(x_ref, gate_ref, omla_ref, od0, od1, od2, ls0, ls1, ls2, odiff_ref,
                  wbm_ref, wbd_ref, wbf_ref, wo_ref, pg_ref, gn_ref, o_ref, hn_ref, scr_ref):
    tm = x_ref.shape[0]
    nch = DIL_OUT // LANES

    def natural(ref, dil):
        if dil == 1:
            return ref[...]
        rows = tm // dil
        for r in range(dil):
            for c in range(nch):
                col = (r * nch + c) * LANES
                scr_ref[c, pl.ds(r, rows, stride=dil), :] = ref[:, col:col + LANES].astype(F32)
        return jnp.concatenate([scr_ref[c] for c in range(nch)], axis=1)

    dils = [dl for _, dl in DIL_CONFIGS]
    l0, l1, l2 = (natural(r_, dl) for r_, dl in zip((ls0, ls1, ls2), dils))
    mx = jnp.maximum(jnp.maximum(l0, l1), l2)
    e0, e1, e2 = jnp.exp2(l0 - mx), jnp.exp2(l1 - mx), jnp.exp2(l2 - mx)
    den = e0 + e1 + e2
    num = e0 * natural(od0, dils[0])
    num = num + e1 * natural(od1, dils[1])
    num = num + e2 * natural(od2, dils[2])
    odil = (num / den).astype(BF16)
    d = D_MODEL
    sub = tm // MERGE_SPLIT
    for part in range(MERGE_SPLIT):
        rs = slice(part * sub, (part + 1) * sub)
        sig = lambda j: jax.nn.sigmoid(gate_ref[rs, j * d:(j + 1) * d].astype(F32))
        y = sig(0) * jnp.dot(omla_ref[rs, :], wbm_ref[...], preferred_element_type=F32)
        y = y + sig(1) * jnp.dot(odil[rs], wbd_ref[...], preferred_element_type=F32)
        y = y + sig(2) * jnp.dot(odiff_ref[rs, :], wbf_ref[...], preferred_element_type=F32)
        z = jnp.dot(y.astype(BF16), wo_ref[...], preferred_element_type=F32)
        xn = x_ref[rs, :] + _rms(z, pg_ref[...])
        o_ref[rs, :] = xn
        hn_ref[rs, :] = _rms(xn, gn_ref[...]).astype(hn_ref.dtype)


def _merge(x2, gates, omla, odil, lses, odiff, wbm, wbd, wbf, wo, pg, g_next):
    t, d = x2.shape
    tm = TM_MERGE
    row = lambda a: pl.BlockSpec((tm * a.shape[0] // t, a.shape[1]), lambda i: (i, 0))
    acts = [x2, gates, omla, *odil, *lses, odiff]
    wts = [wbm, wbd, wbf, wo, pg, g_next]
    return pl.pallas_call(
        _merge_kernel,
        grid=(t // tm,),
        in_specs=[row(a) for a in acts] + [_resident(a) for a in wts],
        out_specs=[pl.BlockSpec((tm, d), lambda i: (i, 0))] * 2,
        out_shape=[jax.ShapeDtypeStruct((t, d), F32), jax.ShapeDtypeStruct((t, d), BF16)],
        scratch_shapes=[pltpu.VMEM((DIL_OUT // LANES, tm, LANES), F32)],
        name="merge",
        compiler_params=_cparams(("parallel",)),
    )(*acts, *wts)


def _ffn_kernel(hp_ref, h_ref, hn_ref, x_ref, wup_ref, cw_ref, wd_ref, pg_ref, *rest, tm, seq, has_next):
    if has_next:
        gn_ref, o_ref, hnext_ref, hext_ref, a_ref = rest
    else:
        o_ref, hext_ref, a_ref = rest
    i = pl.program_id(0)
    hl = FFN_HALO
    keep_p = (i * tm) % seq != 0
    keep_n = ((i + 1) * tm) % seq != 0
    hext_ref[0:hl, :] = jnp.where(keep_p, hp_ref[...], jnp.zeros_like(hp_ref))
    hext_ref[hl:hl + tm, :] = h_ref[...]
    hext_ref[hl + tm:2 * hl + tm, :] = jnp.where(keep_n, hn_ref[...], jnp.zeros_like(hn_ref))
    hx = hext_ref[...]
    n_ext = tm + 2 * hl

    def conv(u, cw):
        up = pltpu.roll(u, 1, 0)
        dn = pltpu.roll(u, n_ext - 1, 0)
        z = cw[0:1] * up + cw[1:2] * u + cw[2:3] * dn
        return z[hl:hl + tm]

    for c in range(D_FF // TF_FFN):
        sg = slice(c * TF_FFN, (c + 1) * TF_FFN)
        sv = slice(D_FF + c * TF_FFN, D_FF + (c + 1) * TF_FFN)
        ug = conv(jnp.dot(hx, wup_ref[:, sg], preferred_element_type=F32), cw_ref[:, sg])
        uv = conv(jnp.dot(hx, wup_ref[:, sv], preferred_element_type=F32), cw_ref[:, sv])
        a_ref[:, sg] = (jax.nn.gelu(ug, approximate=True) * uv).astype(BF16)
    y = jnp.dot(a_ref[...], wd_ref[...], preferred_element_type=F32)
    xn = x_ref[...] + _rms(y, pg_ref[...])
    o_ref[...] = xn
    if has_next:
        hnext_ref[...] = _rms(xn, gn_ref[...]).astype(hnext_ref.dtype)


def _ffn(x2, h, wup, convw, wdown, pg, g_next, seq):
    t, d = x2.shape
    tm = min(TM_FFN, seq)
    hl = FFN_HALO
    rb = tm // hl
    nhb = t // hl
    has_next = g_next is not None
    row = pl.BlockSpec((tm, d), lambda i: (i, 0))
    wts = [wup, convw, wdown, pg] + ([g_next] if has_next else [])
    outs = [jax.ShapeDtypeStruct((t, d), F32)] + ([jax.ShapeDtypeStruct((t, d), BF16)] if has_next else [])
    res = pl.pallas_call(
        functools.partial(_ffn_kernel, tm=tm, seq=seq, has_next=has_next),
        grid=(t // tm,),
        in_specs=[
            pl.BlockSpec((hl, d), lambda i: (jnp.maximum(i * rb - 1, 0), 0)),
            row,
            pl.BlockSpec((hl, d), lambda i: (jnp.minimum((i + 1) * rb, nhb - 1), 0)),
            row,
        ] + [_resident(a) for a in wts],
        out_specs=[row] * len(outs),
        out_shape=outs,
        scratch_shapes=[pltpu.VMEM((tm + 2 * hl, d), BF16), pltpu.VMEM((tm, D_FF), BF16)],
        name="conv_ffn",
        compiler_params=_cparams(("parallel",)),
    )(h, h, h, x2, *wts)
    return (res[0], res[1]) if has_next else (res[0], None)


def _rope_tables(positions, rot_dim, period, base):
    half = rot_dim // 2
    inv = ROPE_THETA ** (-jnp.arange(0, rot_dim, 2, dtype=F32) / rot_dim)
    ang = positions.astype(F32).reshape(-1, 1) * inv[None, :]
    cos, sin = jnp.cos(ang), jnp.sin(ang)
    lane = jnp.arange(LANES)
    rel = lane % period - base
    idx = jnp.clip(rel, 0, rot_dim - 1) % half
    first = (rel >= 0) & (rel < half)
    second = (rel >= half) & (rel < rot_dim)
    cg, sg = cos[:, idx], sin[:, idx]
    c = jnp.where((first | second)[None, :], cg, 1.0)
    sa = jnp.where(first[None, :], -sg, 0.0)
    sb = jnp.where(second[None, :], sg, 0.0)
    return c, sa, sb


def _layer_weights(l, w_in, mla_w_uq, mla_w_ukv, w_branch_mla):
    d = D_MODEL
    wi = w_in[l]
    o1 = GATE_COLS
    o2 = o1 + MLA_COLS
    o3 = o2 + DIL_COLS
    w_gate = wi[:, :o1].astype(BF16)
    wm = wi[:, o1:o2]
    z = lambda n: jnp.zeros((d, n), F32)
    w_mla_in = jnp.concatenate(
        [wm[:, :MLA_Q_RANK + MLA_KV_RANK], z(MLA_NOPE), wm[:, MLA_Q_RANK + MLA_KV_RANK:], z(LANES - MLA_NOPE - MLA_ROPE)],
        axis=1).astype(BF16)
    qscale = LOG2E * HEAD_DIM ** -0.5
    wd = wi[:, o2:o3].reshape(d, DIL_GROUPS, 3, DIL_OUT)
    wd = wd * jnp.array([qscale, 1.0, 1.0], F32)[None, None, :, None]
    w_dil = wd.reshape(d, DIL_COLS).astype(BF16)
    wf = wi[:, o3:].reshape(d, 3, DIFF_OUT)
    wf = wf * jnp.array([qscale, 1.0, 1.0], F32)[None, :, None]
    w_diff = wf.reshape(d, DIFF_COLS).astype(BF16)
    qk = MLA_NOPE + MLA_ROPE
    wq = mla_w_uq[l].reshape(MLA_Q_RANK, MLA_HEADS, qk) * (LOG2E * qk ** -0.5)
    wq = jnp.pad(wq, ((0, 0), (0, 0), (0, LANES - qk))).reshape(MLA_Q_RANK, MLA_HEADS * LANES).astype(BF16)
    wkv = mla_w_ukv[l].reshape(MLA_KV_RANK, MLA_HEADS, MLA_NOPE + MLA_V)
    pad_head = lambda w_: jnp.pad(w_, ((0, 0), (0, 0), (0, LANES - w_.shape[2]))).reshape(
        MLA_KV_RANK, MLA_HEADS * LANES).astype(BF16)
    wk = pad_head(wkv[:, :, :MLA_NOPE])
    wv = pad_head(wkv[:, :, MLA_NOPE:])
    wb = w_branch_mla[l].reshape(MLA_HEADS, MLA_V, d)
    wb = jnp.pad(wb, ((0, 0), (0, LANES - MLA_V), (0, 0))).reshape(MLA_HEADS * LANES, d).astype(BF16)
    return w_gate, w_mla_in, w_dil, w_diff, wq, wk, wv, wb


def _lambda_init(layer):
    return 0.8 - 0.6 * math.exp(-0.3 * layer)


def kernel(x, positions, attn_pre_norm, w_in, mla_q_norm, mla_w_uq, mla_kv_norm, mla_w_ukv, diff_lambda,
           diff_subln, w_branch_mla, w_branch_dil, w_branch_diff, w_out, attn_post_norm, ffn_pre_norm,
           w_up, ffn_conv, w_down, ffn_post_norm):
    b, s, d = x.shape
    t = b * s
    depth = w_in.shape[0]
    tab_p = _rope_tables(positions, PARTIAL_ROT_DIM, HEAD_DIM, 0)
    tab_m = _rope_tables(positions, MLA_ROPE, LANES, MLA_NOPE)
    x2 = x.reshape(t, d)
    row = lambda a: a.reshape(1, -1)
    h = _norm_cast(x2, row(attn_pre_norm[0]))
    for l in range(depth):
        w_gate, w_mla_in, w_dil, w_diff, wq, wk, wv, wbm = _layer_weights(l, w_in, mla_w_uq, mla_w_ukv, w_branch_mla)
        gates = _inproj(h, w_gate, None, tn=GATE_COLS // 2, rope_chunks=0, tm=TM_GATES)
        gw = DIL_COLS // DIL_GROUPS
        p_dil =[_inproj(h, w_dil[:, gi * gw:(gi + 1) * gw], tab_p, tn=gw, rope_chunks=2 * DIL_OUT // LANES, dil=dl)
                 for gi, (_, dl) in enumerate(DIL_CONFIGS)]
        p_diff = _inproj(h, w_diff, tab_p, tn=DIFF_COLS, rope_chunks=2 * DIFF_OUT // LANES)
        q_m, k_m, v_m = _mla_prep(h, w_mla_in, row(mla_q_norm[l]), row(mla_kv_norm[l]), wq, wk, wv, tab_m)
        hw = MLA_HEADS * LANES
        o_mla = _flash_mla(q_m.reshape(b, s, hw), k_m.reshape(b, s, hw), v_m.reshape(b, s, hw)).reshape(t, hw)
        o_diff = _flash_diff(p_diff.reshape(b, s, DIFF_COLS), diff_lambda[l], row(diff_subln[l]),
                             _lambda_init(l)).reshape(t, DIFF_OUT)
        dil = [_dilated(p_dil[gi], dl, b, s) for gi, (_, dl) in enumerate(DIL_CONFIGS)]
        x2, h_ffn = _merge(x2, gates, o_mla, [o for o, _ in dil], [ls for _, ls in dil], o_diff,
                           wbm, w_branch_dil[l].astype(BF16), w_branch_diff[l].astype(BF16), w_out[l].astype(BF16),
                           row(attn_post_norm[l]), row(ffn_pre_norm[l]))
        g_next = row(attn_pre_norm[l + 1]) if l + 1 < depth else None
        x2, h = _ffn(x2, h_ffn, w_up[l].astype(BF16), ffn_conv[l], w_down[l].astype(BF16),
                     row(ffn_post_norm[l]), g_next, s)
    return x2.reshape(b, s, d)
```

```python
import functools
import math

import jax
import jax.numpy as jnp
from jax import lax
from jax.experimental import pallas as pl
from jax.experimental.pallas import tpu as pltpu

F32 = jnp.float32
BF16 = jnp.bfloat16

D_MODEL = 1024
ROPE_THETA = 500000.0
NORM_EPS = 1e-6

MLA_HEADS = 8
MLA_NOPE = 64
MLA_ROPE = 32
MLA_V = 64
MLA_Q_RANK = 384
MLA_KV_RANK = 256

HEAD_DIM = 64
PARTIAL_ROT_DIM = HEAD_DIM // 4
DIL_CONFIGS = ((128, 1), (512, 4), (2048, 16))
DIL_GROUPS = len(DIL_CONFIGS)
DIL_HEADS = 6
DIFF_HEADS = 4
D_FF = 2816
CONV_WIDTH = 3

GATE_COLS = 3 * D_MODEL
MLA_COLS = MLA_Q_RANK + MLA_KV_RANK + MLA_ROPE
DIL_COLS = DIL_GROUPS * 3 * DIL_HEADS * HEAD_DIM
DIFF_COLS = 3 * DIFF_HEADS * 2 * HEAD_DIM
DIL_OUT = DIL_HEADS * HEAD_DIM
DIFF_OUT = DIFF_HEADS * 2 * HEAD_DIM

LANES = 128
LOG2E = 1.4426950408889634
NEG_BIG = -1e30
VMEM_LIMIT = 56 * 1024 * 1024

TM_NORM = 1024
TM_PROJ = 1024
TM_GATES = 1024
TM_MERGE = 512
MERGE_SPLIT = 2
TM_FFN = 512
TF_FFN = 256
FFN_HALO = 16
TQ_MLA = 2048
TQ_DIFF = 1024
TK_MLA = 1024
TK_DIFF = 512
TJ_DIL = 512
DIL_RADIUS = 64
DIL_QSUB = 128
DIL_KWIN = DIL_QSUB + 2 * DIL_RADIUS


def _cparams(sem):
    return pltpu.CompilerParams(dimension_semantics=sem, vmem_limit_bytes=VMEM_LIMIT)


def _rms(x, g):
    ms = jnp.mean(x * x, axis=-1, keepdims=True)
    return x * lax.rsqrt(ms + NORM_EPS) * g


def _rope128(z, c, sa, sb, shift):
    return z * c + pltpu.roll(z, LANES - shift, 1) * sa + pltpu.roll(z, shift, 1) * sb


def _resident(a):
    return pl.BlockSpec(a.shape, lambda *_: (0,) * a.ndim, pipeline_mode=pl.Buffered(1))


def _norm_kernel(x_ref, g_ref, o_ref):
    o_ref[...] = _rms(x_ref[...], g_ref[...]).astype(o_ref.dtype)


def _norm_cast(x2, g):
    t, d = x2.shape
    tm = TM_NORM
    return pl.pallas_call(
        _norm_kernel,
        grid=(t // tm,),
        in_specs=[pl.BlockSpec((tm, d), lambda i: (i, 0)), _resident(g)],
        out_specs=pl.BlockSpec((tm, d), lambda i: (i, 0)),
        out_shape=jax.ShapeDtypeStruct((t, d), BF16),
        name="pre_norm",
        compiler_params=_cparams(("parallel",)),
    )(x2, g)


def _inproj_kernel(h_ref, w_ref, *rest, rope_chunks, n_chunks, dil):
    rest = list(rest)
    scr_ref = rest.pop() if dil > 1 else None
    o_ref = rest.pop()
    y = jnp.dot(h_ref[...], w_ref[...], preferred_element_type=F32)
    for c in range(n_chunks):
        yc = y[:, c * LANES:(c + 1) * LANES]
        if c < rope_chunks:
            c_ref, sa_ref, sb_ref = rest
            yc = _rope128(yc, c_ref[...], sa_ref[...], sb_ref[...], PARTIAL_ROT_DIM // 2)
        if dil > 1:
            scr_ref[c] = yc
        else:
            o_ref[:, c * LANES:(c + 1) * LANES] = yc.astype(o_ref.dtype)
    if dil > 1:
        rows = y.shape[0] // dil
        for r in range(dil):
            for c in range(n_chunks):
                col = (r * n_chunks + c) * LANES
                o_ref[:, col:col + LANES] = scr_ref[c, pl.ds(r, rows, stride=dil), :].astype(o_ref.dtype)


def _inproj(h, w, tables, *, tn, rope_chunks, dil=1, tm=TM_PROJ):
    t, d = h.shape
    n = w.shape[1]
    assert dil == 1 or n == tn
    in_specs = [
        pl.BlockSpec((tm, d), lambda j, i: (i, 0)),
        pl.BlockSpec((d, tn), lambda j, i: (0, j)),
    ]
    args = [h, w]
    if rope_chunks:
        in_specs += [pl.BlockSpec((tm, LANES), lambda j, i: (i, 0))] * 3
        args += list(tables)
    if dil > 1:
        out_spec = pl.BlockSpec((tm // dil, dil * tn), lambda j, i: (i, 0))
        out_shape = jax.ShapeDtypeStruct((t // dil, dil * tn), BF16)
        scratch = [pltpu.VMEM((tn // LANES, tm, LANES), F32)]
    else:
        out_spec = pl.BlockSpec((tm, tn), lambda j, i: (i, j))
        out_shape = jax.ShapeDtypeStruct((t, n), BF16)
        scratch = []
    return pl.pallas_call(
        functools.partial(_inproj_kernel, rope_chunks=rope_chunks, n_chunks=tn // LANES, dil=dil),
        grid=(n // tn, t // tm),
        in_specs=in_specs,
        out_specs=out_spec,
        out_shape=out_shape,
        scratch_shapes=scratch,
        name=f"inproj_rope_d{dil}" if rope_chunks else "inproj_gates",
        compiler_params=_cparams(("parallel", "parallel")),
    )(*args)


def _mla_prep_kernel(h_ref, win_ref, gq_ref, gkv_ref, wuq_ref, wuk_ref, wuv_ref,
                     c_ref, sa_ref, sb_ref, q_ref, k_ref, v_ref):
    y = jnp.dot(h_ref[...], win_ref[...], preferred_element_type=F32)
    cq = y[:, :MLA_Q_RANK]
    ckv = y[:, MLA_Q_RANK:MLA_Q_RANK + MLA_KV_RANK]
    kr = y[:, MLA_Q_RANK + MLA_KV_RANK:]
    qn = _rms(cq, gq_ref[...]).astype(BF16)
    kvn = _rms(ckv, gkv_ref[...]).astype(BF16)
    q = jnp.dot(qn, wuq_ref[...], preferred_element_type=F32)
    kn = jnp.dot(kvn, wuk_ref[...], preferred_element_type=F32)
    vv = jnp.dot(kvn, wuv_ref[...], preferred_element_type=F32)
    c, sa, sb = c_ref[...], sa_ref[...], sb_ref[...]
    krr = _rope128(kr, c, sa, sb, MLA_ROPE // 2)
    lane = lax.broadcasted_iota(jnp.int32, krr.shape, 1)
    for hd in range(MLA_HEADS):
        sl = slice(hd * LANES, (hd + 1) * LANES)
        q_ref[:, sl] = _rope128(q[:, sl], c, sa, sb, MLA_ROPE // 2).astype(BF16)
        k_ref[:, sl] = jnp.where(lane < MLA_NOPE, kn[:, sl], krr).astype(BF16)
        v_ref[:, sl] = jnp.where(lane < MLA_V, vv[:, sl], 1.0).astype(BF16)


def _mla_prep(h, win, gq, gkv, wuq, wuk, wuv, tables):
    t, d = h.shape
    tm = TM_PROJ
    hw = MLA_HEADS * LANES
    row = lambda w_: pl.BlockSpec((tm, w_), lambda i: (i, 0))
    out = jax.ShapeDtypeStruct((t, hw), BF16)
    wts = [win, gq, gkv, wuq, wuk, wuv]
    return pl.pallas_call(
        _mla_prep_kernel,
        grid=(t // tm,),
        in_specs=[row(d)] + [_resident(a) for a in wts] + [row(LANES)] * 3,
        out_specs=[row(hw)] * 3,
        out_shape=[out] * 3,
        name="mla_prep",
        compiler_params=_cparams(("parallel",)),
    )(h, *wts, *tables)


def _flash_sweep(q, k_at, v_at, m_ref, acc_ref, *, seq, tk):
    nv = acc_ref.shape[1]
    for c in range(seq // tk):
        sl = slice(c * tk, (c + 1) * tk)
        s = lax.dot_general(q, k_at(sl), (((1,), (1,)), ((), ())), preferred_element_type=F32)
        m_cur = jnp.max(s, axis=1, keepdims=True)
        if c == 0:
            m_new = jnp.broadcast_to(m_cur, m_ref.shape)
        else:
            m_prev = m_ref[...]
            m_new = jnp.maximum(m_prev, m_cur)
            alpha = jnp.exp2(m_prev - m_new)
        p = jnp.exp2((s - jnp.tile(m_new, (1, tk // LANES))).astype(BF16))
        pv = jnp.dot(p, v_at(sl), preferred_element_type=F32)
        acc_ref[...] = pv if c == 0 else jnp.tile(alpha, (1, nv // LANES)) * acc_ref[...] + pv
        m_ref[...] = m_new


def _flash_mla_kernel(q_ref, k_ref, v_ref, o_ref, m_ref, acc_ref, *, tk):
    _flash_sweep(q_ref[0], lambda sl: k_ref[0, sl, :], lambda sl: v_ref[0, sl, :], m_ref, acc_ref,
                 seq=k_ref.shape[1], tk=tk)
    acc = acc_ref[...]
    lane = lax.broadcasted_iota(jnp.int32, acc.shape, 1)
    o_ref[0] = jnp.where(lane < MLA_V, acc / pltpu.roll(acc, MLA_V, 1), 0.0).astype(o_ref.dtype)


def _flash_mla(q, k, v):
    b, s, hw = q.shape
    nh = hw // LANES
    tq, tk = min(TQ_MLA, s), min(TK_MLA, s)
    qspec = pl.BlockSpec((1, tq, LANES), lambda bi, h, i: (bi, i, h))
    kspec = pl.BlockSpec((1, s, LANES), lambda bi, h, i: (bi, 0, h))
    return pl.pallas_call(
        functools.partial(_flash_mla_kernel, tk=tk),
        grid=(b, nh, s // tq),
        in_specs=[qspec, kspec, kspec],
        out_specs=qspec,
        out_shape=jax.ShapeDtypeStruct((b, s, hw), BF16),
        scratch_shapes=[pltpu.VMEM((tq, LANES), F32)] * 2,
        name="flash_mla",
        compiler_params=_cparams(("parallel", "parallel", "arbitrary")),
    )(q, k, v)


def _flash_diff_kernel(q_ref, k_ref, v_ref, lam_ref, subln_ref, o_ref, v1_ref, m_ref, acc_ref, *, tk, lam_init):
    @pl.when(pl.program_id(2) == 0)
    def _():
        v1_ref[:, :LANES] = v_ref[0]
        v1_ref[:, LANES:] = jnp.ones((v1_ref.shape[0], LANES), BF16)

    q = q_ref[0]
    tq = q.shape[0]
    lane = lax.broadcasted_iota(jnp.int32, q.shape, 1)
    zero = jnp.zeros_like(q)
    q2 = jnp.concatenate([jnp.where(lane < HEAD_DIM, q, zero), jnp.where(lane >= HEAD_DIM, q, zero)], axis=0)
    _flash_sweep(q2, lambda sl: k_ref[0, sl, :], lambda sl: v1_ref[sl, :], m_ref, acc_ref,
                 seq=k_ref.shape[1], tk=tk)
    acc = acc_ref[...]
    o12 = acc[:, :LANES] / acc[:, LANES:]
    lv = lam_ref[...]
    lam = (jnp.exp(jnp.sum(lv[0:1] * lv[1:2], axis=1, keepdims=True))
           - jnp.exp(jnp.sum(lv[2:3] * lv[3:4], axis=1, keepdims=True)) + lam_init)
    o = o12[:tq] - lam * o12[tq:]
    o_ref[0] = (_rms(o, subln_ref[...]) * (1.0 - lam_init)).astype(o_ref.dtype)


def _flash_diff(proj, lam_vecs, subln, lam_init):
    b, s, _ = proj.shape
    nh = DIFF_HEADS
    tq, tk = min(TQ_DIFF, s), min(TK_DIFF, s)
    return pl.pallas_call(
        functools.partial(_flash_diff_kernel, tk=tk, lam_init=lam_init),
        grid=(b, nh, s // tq),
        in_specs=[
            pl.BlockSpec((1, tq, LANES), lambda bi, h, i: (bi, i, h)),
            pl.BlockSpec((1, s, LANES), lambda bi, h, i: (bi, 0, nh + h)),
            pl.BlockSpec((1, s, LANES), lambda bi, h, i: (bi, 0, 2 * nh + h)),
            pl.BlockSpec(lam_vecs.shape, lambda bi, h, i: (0, 0)),
            pl.BlockSpec(subln.shape, lambda bi, h, i: (0, 0)),
        ],
        out_specs=pl.BlockSpec((1, tq, LANES), lambda bi, h, i: (bi, i, h)),
        out_shape=jax.ShapeDtypeStruct((b, s, DIFF_OUT), BF16),
        scratch_shapes=[pltpu.VMEM((s, 2 * LANES), BF16), pltpu.VMEM((2 * tq, LANES), F32),
                        pltpu.VMEM((2 * tq, 2 * LANES), F32)],
        name="flash_diff",
        compiler_params=_cparams(("parallel", "parallel", "arbitrary")),
    )(proj, proj, proj, lam_vecs, subln)


def _dil_kernel(q_ref, kp_ref, kc_ref, kn_ref, vp_ref, vc_ref, vn_ref, o_ref, lse_ref, *, tj, lsub):
    j0 = pl.program_id(2) * tj
    r = DIL_RADIUS
    qs = DIL_QSUB
    row = lax.broadcasted_iota(jnp.int32, (2 * qs, DIL_KWIN), 0) & (qs - 1)
    col = lax.broadcasted_iota(jnp.int32, (2 * qs, DIL_KWIN), 1)
    band = jnp.abs(col - r - row) <= r
    biases = []
    for i in range(tj // qs):
        kpos = j0 + (i * qs - r) + col
        biases.append(jnp.where(band & (kpos >= 0) & (kpos < lsub), 0.0, NEG_BIG))
    low = lax.broadcasted_iota(jnp.int32, (qs, LANES), 1) < HEAD_DIM
    ones = jnp.ones((tj + 2 * r, LANES), BF16)
    for c in range(DIL_OUT // LANES):
        sl = slice(c * LANES, (c + 1) * LANES)
        kcat = jnp.concatenate([kp_ref[0, :, sl], kc_ref[0, :, sl], kn_ref[0, :, sl]], axis=0)
        vcat = jnp.concatenate([vp_ref[0, :, sl], vc_ref[0, :, sl], vn_ref[0, :, sl]], axis=0)
        vcat = jnp.concatenate([vcat, ones], axis=1)
        for i in range(tj // qs):
            rows = slice(i * qs, (i + 1) * qs)
            q = q_ref[0, rows, sl]
            zero = jnp.zeros_like(q)
            q2 = jnp.concatenate([jnp.where(low, q, zero), jnp.where(low, zero, q)], axis=0)
            kw = kcat[i * qs:i * qs + DIL_KWIN]
            vw = vcat[i * qs:i * qs + DIL_KWIN]
            s = lax.dot_general(q2, kw, (((1,), (1,)), ((), ())), preferred_element_type=F32) + biases[i]
            m = jnp.max(s, axis=1, keepdims=True)
            p = jnp.exp2((s - m).astype(BF16))
            pv = jnp.dot(p, vw, preferred_element_type=F32)
            den = pv[:, LANES:]
            o2 = pv[:, :LANES] / den
            lse2 = m + jnp.log2(den)
            o_ref[0, rows, sl] = jnp.where(low, o2[:qs], o2[qs:]).astype(o_ref.dtype)
            lse_ref[0, rows, sl] = jnp.where(low, lse2[:qs], lse2[qs:])


def _dilated(proj, dil, b, s):
    w = DIL_OUT
    lsub = s // dil
    tj = min(TJ_DIL, lsub)
    nblk = lsub // tj
    view = proj.reshape(b, lsub, dil * 3 * w)

    hr = DIL_RADIUS
    nhalo = lsub // hr

    def spec(t, shift):
        if shift == 0:
            return pl.BlockSpec((1, tj, w), lambda bi, ri, ji: (bi, ji, ri * 3 + t))
        if shift < 0:
            return pl.BlockSpec((1, hr, w), lambda bi, ri, ji: (bi, jnp.maximum(ji * (tj // hr) - 1, 0), ri * 3 + t))
        return pl.BlockSpec((1, hr, w), lambda bi, ri, ji: (bi, jnp.minimum((ji + 1) * (tj // hr), nhalo - 1), ri * 3 + t))

    ospec = pl.BlockSpec((1, tj, w), lambda bi, ri, ji: (bi, ji, ri))
    o, lse = pl.pallas_call(
        functools.partial(_dil_kernel, tj=tj, lsub=lsub),
        grid=(b, dil, nblk),
        in_specs=[spec(0, 0), spec(1, -1), spec(1, 0), spec(1, 1), spec(2, -1), spec(2, 0), spec(2, 1)],
        out_specs=[ospec, ospec],
        out_shape=[jax.ShapeDtypeStruct((b, lsub, dil * w), BF16), jax.ShapeDtypeStruct((b, lsub, dil * w), F32)],
        name=f"dilated_d{dil}",
        compiler_params=_cparams(("parallel", "parallel", "parallel")),
    )(view, view, view, view, view, view, view)
    return o.reshape(b * lsub, dil * w), lse.reshape(b * lsub, dil * w)


def _merge_kernel(x_ref, gate_ref, omla_ref, od0, od1, od2, ls0, ls1, ls2, odiff_ref,
                  wbm_ref, wbd_ref, wbf_ref, wo_ref, pg_ref, gn_ref, o_ref, hn_ref, scr_ref):
    tm = x_ref.shape[0]
    nch = DIL_OUT // LANES

    def natural(ref, dil):
        if dil == 1:
            return ref[...]
        rows = tm // dil
        for r in range(dil):
            for c in range(nch):
                col = (r * nch + c) * LANES
                scr_ref[c, pl.ds(r, rows, stride=dil), :] = ref[:, col:col + LANES].astype(F32)
        return jnp.concatenate([scr_ref[c] for c in range(nch)], axis=1)

    dils = [dl for _, dl in DIL_CONFIGS]
    l0, l1, l2 = (natural(r_, dl) for r_, dl in zip((ls0, ls1, ls2), dils))
    mx = jnp.maximum(jnp.maximum(l0, l1), l2)
    e0, e1, e2 = jnp.exp2(l0 - mx), jnp.exp2(l1 - mx), jnp.exp2(l2 - mx)
    den = e0 + e1 + e2
    num = e0 * natural(od0, dils[0])
    num = num + e1 * natural(od1, dils[1])
    num = num + e2 * natural(od2, dils[2])
    odil = (num / den).astype(BF16)
    d = D_MODEL
    sub = tm // MERGE_SPLIT
    for part in range(MERGE_SPLIT):
        rs = slice(part * sub, (part + 1) * sub)
        sig = lambda j: jax.nn.sigmoid(gate_ref[rs, j * d:(j + 1) * d].astype(F32))
        y = sig(0) * jnp.dot(omla_ref[rs, :], wbm_ref[...], preferred_element_type=F32)
        y = y + sig(1) * jnp.dot(odil[rs], wbd_ref[...], preferred_element_type=F32)
        y = y + sig(2) * jnp.dot(odiff_ref[rs, :], wbf_ref[...], preferred_element_type=F32)
        z = jnp.dot(y.astype(BF16), wo_ref[...], preferred_element_type=F32)
        xn = x_ref[rs, :] + _rms(z, pg_ref[...])
        o_ref[rs, :] = xn
        hn_ref[rs, :] = _rms(xn, gn_ref[...]).astype(hn_ref.dtype)


def _merge(x2, gates, omla, odil, lses, odiff, wbm, wbd, wbf, wo, pg, g_next):
    t, d = x2.shape
    tm = TM_MERGE
    row = lambda a: pl.BlockSpec((tm * a.shape[0] // t, a.shape[1]), lambda i: (i, 0))
    acts = [x2, gates, omla, *odil, *lses, odiff]
    wts = [wbm, wbd, wbf, wo, pg, g_next]
    return pl.pallas_call(
        _merge_kernel,
        grid=(t // tm,),
        in_specs=[row(a) for a in acts] + [_resident(a) for a in wts],
        out_specs=[pl.BlockSpec((tm, d), lambda i: (i, 0))] * 2,
        out_shape=[jax.ShapeDtypeStruct((t, d), F32), jax.ShapeDtypeStruct((t, d), BF16)],
        scratch_shapes=[pltpu.VMEM((DIL_OUT // LANES, tm, LANES), F32)],
        name="merge",
        compiler_params=_cparams(("parallel",)),
    )(*acts, *wts)


def _ffn_kernel(hp_ref, h_ref, hn_ref, x_ref, wup_ref, cw_ref, wd_ref, pg_ref, *rest, tm, seq, has_next):
    if has_next:
        gn_ref, o_ref, hnext_ref, hext_ref, a_ref = rest
    else:
        o_ref, hext_ref, a_ref = rest
    i = pl.program_id(0)
    hl = FFN_HALO
    keep_p = (i * tm) % seq != 0
    keep_n = ((i + 1) * tm) % seq != 0
    hext_ref[0:hl, :] = jnp.where(keep_p, hp_ref[...], jnp.zeros_like(hp_ref))
    hext_ref[hl:hl + tm, :] = h_ref[...]
    hext_ref[hl + tm:2 * hl + tm, :] = jnp.where(keep_n, hn_ref[...], jnp.zeros_like(hn_ref))
    hx = hext_ref[...]
    n_ext = tm + 2 * hl

    def conv(u, cw):
        up = pltpu.roll(u, 1, 0)
        dn = pltpu.roll(u, n_ext - 1, 0)
        z = cw[0:1] * up + cw[1:2] * u + cw[2:3] * dn
        return z[hl:hl + tm]

    for c in range(D_FF // TF_FFN):
        sg = slice(c * TF_FFN, (c + 1) * TF_FFN)
        sv = slice(D_FF + c * TF_FFN, D_FF + (c + 1) * TF_FFN)
        ug = conv(jnp.dot(hx, wup_ref[:, sg], preferred_element_type=F32), cw_ref[:, sg])
        uv = conv(jnp.dot(hx, wup_ref[:, sv], preferred_element_type=F32), cw_ref[:, sv])
        a_ref[:, sg] = (jax.nn.gelu(ug, approximate=True) * uv).astype(BF16)
    y = jnp.dot(a_ref[...], wd_ref[...], preferred_element_type=F32)
    xn = x_ref[...] + _rms(y, pg_ref[...])
    o_ref[...] = xn
    if has_next:
        hnext_ref[...] = _rms(xn, gn_ref[...]).astype(hnext_ref.dtype)


def _ffn(x2, h, wup, convw, wdown, pg, g_next, seq):
    t, d = x2.shape
    tm = min(TM_FFN, seq)
    hl = FFN_HALO
    rb = tm // hl
    nhb = t // hl
    has_next = g_next is not None
    row = pl.BlockSpec((tm, d), lambda i: (i, 0))
    wts = [wup, convw, wdown, pg] + ([g_next] if has_next else [])
    outs = [jax.ShapeDtypeStruct((t, d), F32)] + ([jax.ShapeDtypeStruct((t, d), BF16)] if has_next else [])
    res = pl.pallas_call(
        functools.partial(_ffn_kernel, tm=tm, seq=seq, has_next=has_next),
        grid=(t // tm,),
        in_specs=[
            pl.BlockSpec((hl, d), lambda i: (jnp.maximum(i * rb - 1, 0), 0)),
            row,
            pl.BlockSpec((hl, d), lambda i: (jnp.minimum((i + 1) * rb, nhb - 1), 0)),
            row,
        ] + [_resident(a) for a in wts],
        out_specs=[row] * len(outs),
        out_shape=outs,
        scratch_shapes=[pltpu.VMEM((tm + 2 * hl, d), BF16), pltpu.VMEM((tm, D_FF), BF16)],
        name="conv_ffn",
        compiler_params=_cparams(("parallel",)),
    )(h, h, h, x2, *wts)
    return (res[0], res[1]) if has_next else (res[0], None)


def _rope_tables(positions, rot_dim, period, base):
    half = rot_dim // 2
    inv = ROPE_THETA ** (-jnp.arange(0, rot_dim, 2, dtype=F32) / rot_dim)
    ang = positions.astype(F32).reshape(-1, 1) * inv[None, :]
    cos, sin = jnp.cos(ang), jnp.sin(ang)
    lane = jnp.arange(LANES)
    rel = lane % period - base
    idx = jnp.clip(rel, 0, rot_dim - 1) % half
    first = (rel >= 0) & (rel < half)
    second = (rel >= half) & (rel < rot_dim)
    cg, sg = cos[:, idx], sin[:, idx]
    c = jnp.where((first | second)[None, :], cg, 1.0)
    sa = jnp.where(first[None, :], -sg, 0.0)
    sb = jnp.where(second[None, :], sg, 0.0)
    return c, sa, sb


def _layer_weights(l, w_in, mla_w_uq, mla_w_ukv, w_branch_mla):
    d = D_MODEL
    wi = w_in[l]
    o1 = GATE_COLS
    o2 = o1 + MLA_COLS
    o3 = o2 + DIL_COLS
    w_gate = wi[:, :o1].astype(BF16)
    wm = wi[:, o1:o2]
    z = lambda n: jnp.zeros((d, n), F32)
    w_mla_in = jnp.concatenate(
        [wm[:, :MLA_Q_RANK + MLA_KV_RANK], z(MLA_NOPE), wm[:, MLA_Q_RANK + MLA_KV_RANK:], z(LANES - MLA_NOPE - MLA_ROPE)],
        axis=1).astype(BF16)
    qscale = LOG2E * HEAD_DIM ** -0.5
    wd = wi[:, o2:o3].reshape(d, DIL_GROUPS, 3, DIL_OUT)
    wd = wd * jnp.array([qscale, 1.0, 1.0], F32)[None, None, :, None]
    w_dil = wd.reshape(d, DIL_COLS).astype(BF16)
    wf = wi[:, o3:].reshape(d, 3, DIFF_OUT)
    wf = wf * jnp.array([qscale, 1.0, 1.0], F32)[None, :, None]
    w_diff = wf.reshape(d, DIFF_COLS).astype(BF16)
    qk = MLA_NOPE + MLA_ROPE
    wq = mla_w_uq[l].reshape(MLA_Q_RANK, MLA_HEADS, qk) * (LOG2E * qk ** -0.5)
    wq = jnp.pad(wq, ((0, 0), (0, 0), (0, LANES - qk))).reshape(MLA_Q_RANK, MLA_HEADS * LANES).astype(BF16)
    wkv = mla_w_ukv[l].reshape(MLA_KV_RANK, MLA_HEADS, MLA_NOPE + MLA_V)
    pad_head = lambda w_: jnp.pad(w_, ((0, 0), (0, 0), (0, LANES - w_.shape[2]))).reshape(
        MLA_KV_RANK, MLA_HEADS * LANES).astype(BF16)
    wk = pad_head(wkv[:, :, :MLA_NOPE])
    wv = pad_head(wkv[:, :, MLA_NOPE:])
    wb = w_branch_mla[l].reshape(MLA_HEADS, MLA_V, d)
    wb = jnp.pad(wb, ((0, 0), (0, LANES - MLA_V), (0, 0))).reshape(MLA_HEADS * LANES, d).astype(BF16)
    return w_gate, w_mla_in, w_dil, w_diff, wq, wk, wv, wb


def _lambda_init(layer):
    return 0.8 - 0.6 * math.exp(-0.3 * layer)


def kernel(x, positions, attn_pre_norm, w_in, mla_q_norm, mla_w_uq, mla_kv_norm, mla_w_ukv, diff_lambda,
           diff_subln, w_branch_mla, w_branch_dil, w_branch_diff, w_out, attn_post_norm, ffn_pre_norm,
           w_up, ffn_conv, w_down, ffn_post_norm):
    b, s, d = x.shape
    t = b * s
    depth = w_in.shape[0]
    tab_p = _rope_tables(positions, PARTIAL_ROT_DIM, HEAD_DIM, 0)
    tab_m = _rope_tables(positions, MLA_ROPE, LANES, MLA_NOPE)
    x2 = x.reshape(t, d)
    row = lambda a: a.reshape(1, -1)
    h = _norm_cast(x2, row(attn_pre_norm[0]))
    for l in range(depth):
        w_gate, w_mla_in, w_dil, w_diff, wq, wk, wv, wbm = _layer_weights(l, w_in, mla_w_uq, mla_w_ukv, w_branch_mla)
        gates = _inproj(h, w_gate, None, tn=GATE_COLS // 2, rope_chunks=0, tm=TM_GATES)
        gw = DIL_COLS // DIL_GROUPS
        p_dil =[_inproj(h, w_dil[:, gi * gw:(gi + 1) * gw], tab_p, tn=gw, rope_chunks=2 * DIL_OUT // LANES, dil=dl)
                 for gi, (_, dl) in enumerate(DIL_CONFIGS)]
        p_diff = _inproj(h, w_diff, tab_p, tn=DIFF_COLS, rope_chunks=2 * DIFF_OUT // LANES)
        q_m, k_m, v_m = _mla_prep(h, w_mla_in, row(mla_q_norm[l]), row(mla_kv_norm[l]), wq, wk, wv, tab_m)
        hw = MLA_HEADS * LANES
        o_mla = _flash_mla(q_m.reshape(b, s, hw), k_m.reshape(b, s, hw), v_m.reshape(b, s, hw)).reshape(t, hw)
        o_diff = _flash_diff(p_diff.reshape(b, s, DIFF_COLS), diff_lambda[l], row(diff_subln[l]),
                             _lambda_init(l)).reshape(t, DIFF_OUT)
        dil = [_dilated(p_dil[gi], dl, b, s) for gi, (_, dl) in enumerate(DIL_CONFIGS)]
        x2, h_ffn = _merge(x2, gates, o_mla, [o for o, _ in dil], [ls for _, ls in dil], o_diff,
                           wbm, w_branch_dil[l].astype(BF16), w_branch_diff[l].astype(BF16), w_out[l].astype(BF16),
                           row(attn_post_norm[l]), row(ffn_pre_norm[l]))
        g_next = row(attn_pre_norm[l + 1]) if l + 1 < depth else None
        x2, h = _ffn(x2, h_ffn, w_up[l].astype(BF16), ffn_conv[l], w_down[l].astype(BF16),
                     row(ffn_post_norm[l]), g_next, s)
    return x2.reshape(b, s, d)
```

```python
import functools
import math

import jax
import jax.numpy as jnp
from jax import lax
from jax.experimental import pallas as pl
from jax.experimental.pallas import tpu as pltpu

F32 = jnp.float32
BF16 = jnp.bfloat16

D_MODEL = 1024
ROPE_THETA = 500000.0
NORM_EPS = 1e-6

MLA_HEADS = 8
MLA_NOPE = 64
MLA_ROPE = 32
MLA_V = 64
MLA_Q_RANK = 384
MLA_KV_RANK = 256

HEAD_DIM = 64
PARTIAL_ROT_DIM = HEAD_DIM // 4
DIL_CONFIGS = ((128, 1), (512, 4), (2048, 16))
DIL_GROUPS = len(DIL_CONFIGS)
DIL_HEADS = 6
DIFF_HEADS = 4
D_FF = 2816
CONV_WIDTH = 3

GATE_COLS = 3 * D_MODEL
MLA_COLS = MLA_Q_RANK + MLA_KV_RANK + MLA_ROPE
DIL_COLS = DIL_GROUPS * 3 * DIL_HEADS * HEAD_DIM
DIFF_COLS = 3 * DIFF_HEADS * 2 * HEAD_DIM
DIL_OUT = DIL_HEADS * HEAD_DIM
DIFF_OUT = DIFF_HEADS * 2 * HEAD_DIM

LANES = 128
LOG2E = 1.4426950408889634
NEG_BIG = -1e30
VMEM_LIMIT = 56 * 1024 * 1024

TM_NORM = 1024
TM_PROJ = 1024
TM_GATES = 1024
TM_MERGE = 512
MERGE_SPLIT = 2
TM_FFN = 1024
TF_FFN = 256
FFN_HALO = 16
TQ_MLA = 2048
TQ_DIFF = 1024
TK_MLA = 1024
TK_DIFF = 512
TJ_DIL = 512
DIL_RADIUS = 64
DIL_QSUB = 128
DIL_KWIN = DIL_QSUB + 2 * DIL_RADIUS


def _cparams(sem):
    return pltpu.CompilerParams(dimension_semantics=sem, vmem_limit_bytes=VMEM_LIMIT)


def _rms(x, g):
    ms = jnp.mean(x * x, axis=-1, keepdims=True)
    return x * lax.rsqrt(ms + NORM_EPS) * g


def _rope128(z, c, sa, sb, shift):
    return z * c + pltpu.roll(z, LANES - shift, 1) * sa + pltpu.roll(z, shift, 1) * sb


def _resident(a):
    return pl.BlockSpec(a.shape, lambda *_: (0,) * a.ndim, pipeline_mode=pl.Buffered(1))


def _norm_kernel(x_ref, g_ref, o_ref):
    o_ref[...] = _rms(x_ref[...], g_ref[...]).astype(o_ref.dtype)


def _norm_cast(x2, g):
    t, d = x2.shape
    tm = TM_NORM
    return pl.pallas_call(
        _norm_kernel,
        grid=(t // tm,),
        in_specs=[pl.BlockSpec((tm, d), lambda i: (i, 0)), _resident(g)],
        out_specs=pl.BlockSpec((tm, d), lambda i: (i, 0)),
        out_shape=jax.ShapeDtypeStruct((t, d), BF16),
        name="pre_norm",
        compiler_params=_cparams(("parallel",)),
    )(x2, g)


def _inproj_kernel(h_ref, w_ref, *rest, rope_chunks, n_chunks, dil, gate):
    rest = list(rest)
    scr_ref = rest.pop() if dil > 1 else None
    o_ref = rest.pop()
    y = jnp.dot(h_ref[...], w_ref[...], preferred_element_type=F32)
    for c in range(n_chunks):
        yc = y[:, c * LANES:(c + 1) * LANES]
        if gate:
            yc = jax.nn.sigmoid(yc)
        if c < rope_chunks:
            c_ref, sa_ref, sb_ref = rest
            yc = _rope128(yc, c_ref[...], sa_ref[...], sb_ref[...], PARTIAL_ROT_DIM // 2)
        if dil > 1:
            scr_ref[c] = yc
        else:
            o_ref[:, c * LANES:(c + 1) * LANES] = yc.astype(o_ref.dtype)
    if dil > 1:
        rows = y.shape[0] // dil
        for r in range(dil):
            for c in range(n_chunks):
                col = (r * n_chunks + c) * LANES
                o_ref[:, col:col + LANES] = scr_ref[c, pl.ds(r, rows, stride=dil), :].astype(o_ref.dtype)


def _inproj(h, w, tables, *, tn, rope_chunks, dil=1, tm=TM_PROJ, gate=False):
    t, d = h.shape
    n = w.shape[1]
    assert dil == 1 or n == tn
    in_specs = [
        pl.BlockSpec((tm, d), lambda j, i: (i, 0)),
        pl.BlockSpec((d, tn), lambda j, i: (0, j)),
    ]
    args = [h, w]
    if rope_chunks:
        in_specs += [pl.BlockSpec((tm, LANES), lambda j, i: (i, 0))] * 3
        args += list(tables)
    if dil > 1:
        out_spec = pl.BlockSpec((tm // dil, dil * tn), lambda j, i: (i, 0))
        out_shape = jax.ShapeDtypeStruct((t // dil, dil * tn), BF16)
        scratch = [pltpu.VMEM((tn // LANES, tm, LANES), F32)]
    else:
        out_spec = pl.BlockSpec((tm, tn), lambda j, i: (i, j))
        out_shape = jax.ShapeDtypeStruct((t, n), BF16)
        scratch = []
    return pl.pallas_call(
        functools.partial(_inproj_kernel, rope_chunks=rope_chunks, n_chunks=tn // LANES, dil=dil, gate=gate),
        grid=(n // tn, t // tm),
        in_specs=in_specs,
        out_specs=out_spec,
        out_shape=out_shape,
        scratch_shapes=scratch,
        name=f"inproj_rope_d{dil}" if rope_chunks else "inproj_gates",
        compiler_params=_cparams(("parallel", "parallel")),
    )(*args)


def _mla_prep_kernel(h_ref, win_ref, gq_ref, gkv_ref, wuq_ref, wuk_ref, wuv_ref,
                     c_ref, sa_ref, sb_ref, q_ref, k_ref, v_ref):
    y = jnp.dot(h_ref[...], win_ref[...], preferred_element_type=F32)
    cq = y[:, :MLA_Q_RANK]
    ckv = y[:, MLA_Q_RANK:MLA_Q_RANK + MLA_KV_RANK]
    kr = y[:, MLA_Q_RANK + MLA_KV_RANK:]
    qn = _rms(cq, gq_ref[...]).astype(BF16)
    kvn = _rms(ckv, gkv_ref[...]).astype(BF16)
    q = jnp.dot(qn, wuq_ref[...], preferred_element_type=F32)
    kn = jnp.dot(kvn, wuk_ref[...], preferred_element_type=F32)
    vv = jnp.dot(kvn, wuv_ref[...], preferred_element_type=F32)
    c, sa, sb = c_ref[...], sa_ref[...], sb_ref[...]
    krr = _rope128(kr, c, sa, sb, MLA_ROPE // 2)
    lane = lax.broadcasted_iota(jnp.int32, krr.shape, 1)
    for hd in range(MLA_HEADS):
        sl = slice(hd * LANES, (hd + 1) * LANES)
        q_ref[:, sl] = _rope128(q[:, sl], c, sa, sb, MLA_ROPE // 2).astype(BF16)
        k_ref[:, sl] = jnp.where(lane < MLA_NOPE, kn[:, sl], krr).astype(BF16)
        v_ref[:, sl] = jnp.where(lane < MLA_V, vv[:, sl], 1.0).astype(BF16)


def _mla_prep(h, win, gq, gkv, wuq, wuk, wuv, tables):
    t, d = h.shape
    tm = TM_PROJ
    hw = MLA_HEADS * LANES
    row = lambda w_: pl.BlockSpec((tm, w_), lambda i: (i, 0))
    out = jax.ShapeDtypeStruct((t, hw), BF16)
    wts = [win, gq, gkv, wuq, wuk, wuv]
    return pl.pallas_call(
        _mla_prep_kernel,
        grid=(t // tm,),
        in_specs=[row(d)] + [_resident(a) for a in wts] + [row(LANES)] * 3,
        out_specs=[row(hw)] * 3,
        out_shape=[out] * 3,
        name="mla_prep",
        compiler_params=_cparams(("parallel",)),
    )(h, *wts, *tables)


def _flash_sweep(q, k_at, v_at, m_ref, acc_ref, *, seq, tk):
    nv = acc_ref.shape[1]
    for c in range(seq // tk):
        sl = slice(c * tk, (c + 1) * tk)
        s = lax.dot_general(q, k_at(sl), (((1,), (1,)), ((), ())), preferred_element_type=F32)
        m_cur = jnp.max(s, axis=1, keepdims=True)
        if c == 0:
            m_new = jnp.broadcast_to(m_cur, m_ref.shape)
        else:
            m_prev = m_ref[...]
            m_new = jnp.maximum(m_prev, m_cur)
            alpha = jnp.exp2(m_prev - m_new)
        p = jnp.exp2((s - jnp.tile(m_new, (1, tk // LANES))).astype(BF16))
        pv = jnp.dot(p, v_at(sl), preferred_element_type=F32)
        acc_ref[...] = pv if c == 0 else jnp.tile(alpha, (1, nv // LANES)) * acc_ref[...] + pv
        m_ref[...] = m_new


def _flash_mla_kernel(q_ref, k_ref, v_ref, o_ref, m_ref, acc_ref, *, tk):
    _flash_sweep(q_ref[0], lambda sl: k_ref[0, sl, :], lambda sl: v_ref[0, sl, :], m_ref, acc_ref,
                 seq=k_ref.shape[1], tk=tk)
    acc = acc_ref[...]
    lane = lax.broadcasted_iota(jnp.int32, acc.shape, 1)
    o_ref[0] = jnp.where(lane < MLA_V, acc / pltpu.roll(acc, MLA_V, 1), 0.0).astype(o_ref.dtype)


def _flash_mla(q, k, v):
    b, s, hw = q.shape
    nh = hw // LANES
    tq, tk = min(TQ_MLA, s), min(TK_MLA, s)
    qspec = pl.BlockSpec((1, tq, LANES), lambda bi, h, i: (bi, i, h))
    kspec = pl.BlockSpec((1, s, LANES), lambda bi, h, i: (bi, 0, h))
    return pl.pallas_call(
        functools.partial(_flash_mla_kernel, tk=tk),
        grid=(b, nh, s // tq),
        in_specs=[qspec, kspec, kspec],
        out_specs=qspec,
        out_shape=jax.ShapeDtypeStruct((b, s, hw), BF16),
        scratch_shapes=[pltpu.VMEM((tq, LANES), F32)] * 2,
        name="flash_mla",
        compiler_params=_cparams(("parallel", "parallel", "arbitrary")),
    )(q, k, v)


def _flash_diff_kernel(q_ref, k_ref, v_ref, lam_ref, subln_ref, o_ref, v1_ref, m_ref, acc_ref, *, tk, lam_init):
    @pl.when(pl.program_id(2) == 0)
    def _():
        v1_ref[:, :LANES] = v_ref[0]
        v1_ref[:, LANES:] = jnp.ones((v1_ref.shape[0], LANES), BF16)

    q = q_ref[0]
    tq = q.shape[0]
    lane = lax.broadcasted_iota(jnp.int32, q.shape, 1)
    zero = jnp.zeros_like(q)
    q2 = jnp.concatenate([jnp.where(lane < HEAD_DIM, q, zero), jnp.where(lane >= HEAD_DIM, q, zero)], axis=0)
    _flash_sweep(q2, lambda sl: k_ref[0, sl, :], lambda sl: v1_ref[sl, :], m_ref, acc_ref,
                 seq=k_ref.shape[1], tk=tk)
    acc = acc_ref[...]
    o12 = acc[:, :LANES] / acc[:, LANES:]
    lv = lam_ref[...]
    lam = (jnp.exp(jnp.sum(lv[0:1] * lv[1:2], axis=1, keepdims=True))
           - jnp.exp(jnp.sum(lv[2:3] * lv[3:4], axis=1, keepdims=True)) + lam_init)
    o = o12[:tq] - lam * o12[tq:]
    o_ref[0] = (_rms(o, subln_ref[...]) * (1.0 - lam_init)).astype(o_ref.dtype)


def _flash_diff(proj, lam_vecs, subln, lam_init):
    b, s, _ = proj.shape
    nh = DIFF_HEADS
    tq, tk = min(TQ_DIFF, s), min(TK_DIFF, s)
    return pl.pallas_call(
        functools.partial(_flash_diff_kernel, tk=tk, lam_init=lam_init),
        grid=(b, nh, s // tq),
        in_specs=[
            pl.BlockSpec((1, tq, LANES), lambda bi, h, i: (bi, i, h)),
            pl.BlockSpec((1, s, LANES), lambda bi, h, i: (bi, 0, nh + h)),
            pl.BlockSpec((1, s, LANES), lambda bi, h, i: (bi, 0, 2 * nh + h)),
            pl.BlockSpec(lam_vecs.shape, lambda bi, h, i: (0, 0)),
            pl.BlockSpec(subln.shape, lambda bi, h, i: (0, 0)),
        ],
        out_specs=pl.BlockSpec((1, tq, LANES), lambda bi, h, i: (bi, i, h)),
        out_shape=jax.ShapeDtypeStruct((b, s, DIFF_OUT), BF16),
        scratch_shapes=[pltpu.VMEM((s, 2 * LANES), BF16), pltpu.VMEM((2 * tq, LANES), F32),
                        pltpu.VMEM((2 * tq, 2 * LANES), F32)],
        name="flash_diff",
        compiler_params=_cparams(("parallel", "parallel", "arbitrary")),
    )(proj, proj, proj, lam_vecs, subln)


def _dil_kernel(q_ref, kp_ref, kc_ref, kn_ref, vp_ref, vc_ref, vn_ref, o_ref, lse_ref, *, tj, lsub):
    j0 = pl.program_id(2) * tj
    r = DIL_RADIUS
    qs = DIL_QSUB
    row = lax.broadcasted_iota(jnp.int32, (2 * qs, DIL_KWIN), 0) & (qs - 1)
    col = lax.broadcasted_iota(jnp.int32, (2 * qs, DIL_KWIN), 1)
    band = jnp.abs(col - r - row) <= r
    biases = []
    for i in range(tj // qs):
        kpos = j0 + (i * qs - r) + col
        biases.append(jnp.where(band & (kpos >= 0) & (kpos < lsub), 0.0, NEG_BIG))
    low = lax.broadcasted_iota(jnp.int32, (qs, LANES), 1) < HEAD_DIM
    ones = jnp.ones((tj + 2 * r, LANES), BF16)
    for c in range(DIL_OUT // LANES):
        sl = slice(c * LANES, (c + 1) * LANES)
        kcat = jnp.concatenate([kp_ref[0, :, sl], kc_ref[0, :, sl], kn_ref[0, :, sl]], axis=0)
        vcat = jnp.concatenate([vp_ref[0, :, sl], vc_ref[0, :, sl], vn_ref[0, :, sl]], axis=0)
        vcat = jnp.concatenate([vcat, ones], axis=1)
        for i in range(tj // qs):
            rows = slice(i * qs, (i + 1) * qs)
            q = q_ref[0, rows, sl]
            zero = jnp.zeros_like(q)
            q2 = jnp.concatenate([jnp.where(low, q, zero), jnp.where(low, zero, q)], axis=0)
            kw = kcat[i * qs:i * qs + DIL_KWIN]
            vw = vcat[i * qs:i * qs + DIL_KWIN]
            s = lax.dot_general(q2, kw, (((1,), (1,)), ((), ())), preferred_element_type=F32) + biases[i]
            m = jnp.max(s, axis=1, keepdims=True)
            p = jnp.exp2((s - m).astype(BF16))
            pv = jnp.dot(p, vw, preferred_element_type=F32)
            den = pv[:, LANES:]
            o2 = pv[:, :LANES] / den
            lse2 = m + jnp.log2(den)
            o_ref[0, rows, sl] = jnp.where(low, o2[:qs], o2[qs:]).astype(o_ref.dtype)
            lse_ref[0, rows, sl] = jnp.where(low, lse2[:qs], lse2[qs:])


def _dilated(proj, dil, b, s):
    w = DIL_OUT
    lsub = s // dil
    tj = min(TJ_DIL, lsub)
    nblk = lsub // tj
    view = proj.reshape(b, lsub, dil * 3 * w)

    hr = DIL_RADIUS
    nhalo = lsub // hr

    def spec(t, shift):
        if shift == 0:
            return pl.BlockSpec((1, tj, w), lambda bi, ri, ji: (bi, ji, ri * 3 + t))
        if shift < 0:
            return pl.BlockSpec((1, hr, w), lambda bi, ri, ji: (bi, jnp.maximum(ji * (tj // hr) - 1, 0), ri * 3 + t))
        return pl.BlockSpec((1, hr, w), lambda bi, ri, ji: (bi, jnp.minimum((ji + 1) * (tj // hr), nhalo - 1), ri * 3 + t))

    ospec = pl.BlockSpec((1, tj, w), lambda bi, ri, ji: (bi, ji, ri))
    o, lse = pl.pallas_call(
        functools.partial(_dil_kernel, tj=tj, lsub=lsub),
        grid=(b, dil, nblk),
        in_specs=[spec(0, 0), spec(1, -1), spec(1, 0), spec(1, 1), spec(2, -1), spec(2, 0), spec(2, 1)],
        out_specs=[ospec, ospec],
        out_shape=[jax.ShapeDtypeStruct((b, lsub, dil * w), BF16), jax.ShapeDtypeStruct((b, lsub, dil * w), F32)],
        name=f"dilated_d{dil}",
        compiler_params=_cparams(("parallel", "parallel", "parallel")),
    )(view, view, view, view, view, view, view)
    return o.reshape(b * lsub, dil * w), lse.reshape(b * lsub, dil * w)


def _merge_kernel(x_ref, gate_ref, omla_ref, od0, od1, od2, ls0, ls1, ls2, odiff_ref,
                  wbm_ref, wbd_ref, wbf_ref, wo_ref, pg_ref, gn_ref, o_ref, hn_ref, scr_ref):
    tm = x_ref.shape[0]
    nch = DIL_OUT // LANES

    def natural(ref, dil):
        if dil == 1:
            return ref[...]
        rows = tm // dil
        for r in range(dil):
            for c in range(nch):
                col = (r * nch + c) * LANES
                scr_ref[c, pl.ds(r, rows, stride=dil), :] = ref[:, col:col + LANES].astype(F32)
        return jnp.concatenate([scr_ref[c] for c in range(nch)], axis=1)

    dils = [dl for _, dl in DIL_CONFIGS]
    l0, l1, l2 = (natural(r_, dl) for r_, dl in zip((ls0, ls1, ls2), dils))
    mx = jnp.maximum(jnp.maximum(l0, l1), l2)
    e0, e1, e2 = jnp.exp2(l0 - mx), jnp.exp2(l1 - mx), jnp.exp2(l2 - mx)
    den = e0 + e1 + e2
    num = e0 * natural(od0, dils[0])
    num = num + e1 * natural(od1, dils[1])
    num = num + e2 * natural(od2, dils[2])
    odil = (num / den).astype(BF16)
    d = D_MODEL
    sub = tm // MERGE_SPLIT
    for part in range(MERGE_SPLIT):
        rs = slice(part * sub, (part + 1) * sub)
        sig = lambda j: gate_ref[rs, j * d:(j + 1) * d].astype(F32)
        y = sig(0) * jnp.dot(omla_ref[rs, :], wbm_ref[...], preferred_element_type=F32)
        y = y + sig(1) * jnp.dot(odil[rs], wbd_ref[...], preferred_element_type=F32)
        y = y + sig(2) * jnp.dot(odiff_ref[rs, :], wbf_ref[...], preferred_element_type=F32)
        z = jnp.dot(y.astype(BF16), wo_ref[...], preferred_element_type=F32)
        xn = x_ref[rs, :] + _rms(z, pg_ref[...])
        o_ref[rs, :] = xn
        hn_ref[rs, :] = _rms(xn, gn_ref[...]).astype(hn_ref.dtype)


def _merge(x2, gates, omla, odil, lses, odiff, wbm, wbd, wbf, wo, pg, g_next):
    t, d = x2.shape
    tm = TM_MERGE
    row = lambda a: pl.BlockSpec((tm * a.shape[0] // t, a.shape[1]), lambda i: (i, 0))
    acts = [x2, gates, omla, *odil, *lses, odiff]
    wts = [wbm, wbd, wbf, wo, pg, g_next]
    return pl.pallas_call(
        _merge_kernel,
        grid=(t // tm,),
        in_specs=[row(a) for a in acts] + [_resident(a) for a in wts],
        out_specs=[pl.BlockSpec((tm, d), lambda i: (i, 0))] * 2,
        out_shape=[jax.ShapeDtypeStruct((t, d), F32), jax.ShapeDtypeStruct((t, d), BF16)],
        scratch_shapes=[pltpu.VMEM((DIL_OUT // LANES, tm, LANES), F32)],
        name="merge",
        compiler_params=_cparams(("parallel",)),
    )(*acts, *wts)


def _ffn_kernel(hp_ref, h_ref, hn_ref, x_ref, wup_ref, cw_ref, wd_ref, pg_ref, *rest, tm, seq, has_next):
    if has_next:
        gn_ref, o_ref, hnext_ref, hext_ref, a_ref = rest
    else:
        o_ref, hext_ref, a_ref = rest
    i = pl.program_id(0)
    hl = FFN_HALO
    keep_p = (i * tm) % seq != 0
    keep_n = ((i + 1) * tm) % seq != 0
    hext_ref[0:hl, :] = jnp.where(keep_p, hp_ref[...], jnp.zeros_like(hp_ref))
    hext_ref[hl:hl + tm, :] = h_ref[...]
    hext_ref[hl + tm:2 * hl + tm, :] = jnp.where(keep_n, hn_ref[...], jnp.zeros_like(hn_ref))
    hx = hext_ref[...]
    n_ext = tm + 2 * hl

    def conv(u, cw):
        up = pltpu.roll(u, 1, 0)
        dn = pltpu.roll(u, n_ext - 1, 0)
        z = cw[0:1] * up + cw[1:2] * u + cw[2:3] * dn
        return z[hl:hl + tm]

    for c in range(D_FF // TF_FFN):
        sg = slice(c * TF_FFN, (c + 1) * TF_FFN)
        sv = slice(D_FF + c * TF_FFN, D_FF + (c + 1) * TF_FFN)
        ug = conv(jnp.dot(hx, wup_ref[:, sg], preferred_element_type=F32), cw_ref[:, sg])
        uv = conv(jnp.dot(hx, wup_ref[:, sv], preferred_element_type=F32), cw_ref[:, sv])
        a_ref[:, sg] = (jax.nn.gelu(ug, approximate=True) * uv).astype(BF16)
    y = jnp.dot(a_ref[...], wd_ref[...], preferred_element_type=F32)
    xn = x_ref[...] + _rms(y, pg_ref[...])
    o_ref[...] = xn
    if has_next:
        hnext_ref[...] = _rms(xn, gn_ref[...]).astype(hnext_ref.dtype)


def _ffn(x2, h, wup, convw, wdown, pg, g_next, seq):
    t, d = x2.shape
    tm = min(TM_FFN, seq)
    hl = FFN_HALO
    rb = tm // hl
    nhb = t // hl
    has_next = g_next is not None
    row = pl.BlockSpec((tm, d), lambda i: (i, 0))
    wts = [wup, convw, wdown, pg] + ([g_next] if has_next else [])
    outs = [jax.ShapeDtypeStruct((t, d), F32)] + ([jax.ShapeDtypeStruct((t, d), BF16)] if has_next else [])
    res = pl.pallas_call(
        functools.partial(_ffn_kernel, tm=tm, seq=seq, has_next=has_next),
        grid=(t // tm,),
        in_specs=[
            pl.BlockSpec((hl, d), lambda i: (jnp.maximum(i * rb - 1, 0), 0)),
            row,
            pl.BlockSpec((hl, d), lambda i: (jnp.minimum((i + 1) * rb, nhb - 1), 0)),
            row,
        ] + [_resident(a) for a in wts],
        out_specs=[row] * len(outs),
        out_shape=outs,
        scratch_shapes=[pltpu.VMEM((tm + 2 * hl, d), BF16), pltpu.VMEM((tm, D_FF), BF16)],
        name="conv_ffn",
        compiler_params=_cparams(("parallel",)),
    )(h, h, h, x2, *wts)
    return (res[0], res[1]) if has_next else (res[0], None)


def _rope_tables(positions, rot_dim, period, base):
    half = rot_dim // 2
    inv = ROPE_THETA ** (-jnp.arange(0, rot_dim, 2, dtype=F32) / rot_dim)
    ang = positions.astype(F32).reshape(-1, 1) * inv[None, :]
    cos, sin = jnp.cos(ang), jnp.sin(ang)
    lane = jnp.arange(LANES)
    rel = lane % period - base
    first = (rel >= 0) & (rel < half)
    second = (rel >= half) & (rel < rot_dim)
    assert base % half == 0 and period % half == 0
    cg, sg = jnp.tile(cos, (1, LANES // half)), jnp.tile(sin, (1, LANES // half))
    c = jnp.where((first | second)[None, :], cg, 1.0)
    sa = jnp.where(first[None, :], -sg, 0.0)
    sb = jnp.where(second[None, :], sg, 0.0)
    return c, sa, sb


def _layer_weights(l, w_in, mla_w_uq, mla_w_ukv, w_branch_mla):
    d = D_MODEL
    wi = w_in[l]
    o1 = GATE_COLS
    o2 = o1 + MLA_COLS
    o3 = o2 + DIL_COLS
    w_gate = wi[:, :o1].astype(BF16)
    wm = wi[:, o1:o2]
    z = lambda n: jnp.zeros((d, n), F32)
    w_mla_in = jnp.concatenate(
        [wm[:, :MLA_Q_RANK + MLA_KV_RANK], z(MLA_NOPE), wm[:, MLA_Q_RANK + MLA_KV_RANK:], z(LANES - MLA_NOPE - MLA_ROPE)],
        axis=1).astype(BF16)
    qscale = LOG2E * HEAD_DIM ** -0.5
    wd = wi[:, o2:o3].reshape(d, DIL_GROUPS, 3, DIL_OUT)
    wd = wd * jnp.array([qscale, 1.0, 1.0], F32)[None, None, :, None]
    w_dil = wd.reshape(d, DIL_COLS).astype(BF16)
    wf = wi[:, o3:].reshape(d, 3, DIFF_OUT)
    wf = wf * jnp.array([qscale, 1.0, 1.0], F32)[None, :, None]
    w_diff = wf.reshape(d, DIFF_COLS).astype(BF16)
    qk = MLA_NOPE + MLA_ROPE
    wq = mla_w_uq[l].reshape(MLA_Q_RANK, MLA_HEADS, qk) * (LOG2E * qk ** -0.5)
    wq = jnp.pad(wq, ((0, 0), (0, 0), (0, LANES - qk))).reshape(MLA_Q_RANK, MLA_HEADS * LANES).astype(BF16)
    wkv = mla_w_ukv[l].reshape(MLA_KV_RANK, MLA_HEADS, MLA_NOPE + MLA_V)
    pad_head = lambda w_: jnp.pad(w_, ((0, 0), (0, 0), (0, LANES - w_.shape[2]))).reshape(
        MLA_KV_RANK, MLA_HEADS * LANES).astype(BF16)
    wk = pad_head(wkv[:, :, :MLA_NOPE])
    wv = pad_head(wkv[:, :, MLA_NOPE:])
    wb = w_branch_mla[l].reshape(MLA_HEADS, MLA_V, d)
    wb = jnp.pad(wb, ((0, 0), (0, LANES - MLA_V), (0, 0))).reshape(MLA_HEADS * LANES, d).astype(BF16)
    return w_gate, w_mla_in, w_dil, w_diff, wq, wk, wv, wb


def _lambda_init(layer):
    return 0.8 - 0.6 * math.exp(-0.3 * layer)


def kernel(x, positions, attn_pre_norm, w_in, mla_q_norm, mla_w_uq, mla_kv_norm, mla_w_ukv, diff_lambda,
           diff_subln, w_branch_mla, w_branch_dil, w_branch_diff, w_out, attn_post_norm, ffn_pre_norm,
           w_up, ffn_conv, w_down, ffn_post_norm):
    b, s, d = x.shape
    t = b * s
    depth = w_in.shape[0]
    tab_p = _rope_tables(positions, PARTIAL_ROT_DIM, HEAD_DIM, 0)
    tab_m = _rope_tables(positions, MLA_ROPE, LANES, MLA_NOPE)
    x2 = x.reshape(t, d)
    row = lambda a: a.reshape(1, -1)
    h = _norm_cast(x2, row(attn_pre_norm[0]))
    for l in range(depth):
        w_gate, w_mla_in, w_dil, w_diff, wq, wk, wv, wbm = _layer_weights(l, w_in, mla_w_uq, mla_w_ukv, w_branch_mla)
        gates = _inproj(h, w_gate, None, tn=GATE_COLS // 2, rope_chunks=0, tm=TM_GATES, gate=True)
        gw = DIL_COLS // DIL_GROUPS
        p_dil =[_inproj(h, w_dil[:, gi * gw:(gi + 1) * gw], tab_p, tn=gw, rope_chunks=2 * DIL_OUT // LANES, dil=dl)
                 for gi, (_, dl) in enumerate(DIL_CONFIGS)]
        p_diff = _inproj(h, w_diff, tab_p, tn=DIFF_COLS, rope_chunks=2 * DIFF_OUT // LANES)
        q_m, k_m, v_m = _mla_prep(h, w_mla_in, row(mla_q_norm[l]), row(mla_kv_norm[l]), wq, wk, wv, tab_m)
        hw = MLA_HEADS * LANES
        o_mla = _flash_mla(q_m.reshape(b, s, hw), k_m.reshape(b, s, hw), v_m.reshape(b, s, hw)).reshape(t, hw)
        o_diff = _flash_diff(p_diff.reshape(b, s, DIFF_COLS), diff_lambda[l], row(diff_subln[l]),
                             _lambda_init(l)).reshape(t, DIFF_OUT)
        dil = [_dilated(p_dil[gi], dl, b, s) for gi, (_, dl) in enumerate(DIL_CONFIGS)]
        x2, h_ffn = _merge(x2, gates, o_mla, [o for o, _ in dil], [ls for _, ls in dil], o_diff,
                           wbm, w_branch_dil[l].astype(BF16), w_branch_diff[l].astype(BF16), w_out[l].astype(BF16),
                           row(attn_post_norm[l]), row(ffn_pre_norm[l]))
        g_next = row(attn_pre_norm[l + 1]) if l + 1 < depth else None
        x2, h = _ffn(x2, h_ffn, w_up[l].astype(BF16), ffn_conv[l], w_down[l].astype(BF16),
                     row(ffn_post_norm[l]), g_next, s)
    return x2.reshape(b, s, d)
```

```python
import functools
import math

import jax
import jax.numpy as jnp
from jax import lax
from jax.experimental import pallas as pl
from jax.experimental.pallas import tpu as pltpu

F32 = jnp.float32
BF16 = jnp.bfloat16

D_MODEL = 1024
ROPE_THETA = 500000.0
NORM_EPS = 1e-6

MLA_HEADS = 8
MLA_NOPE = 64
MLA_ROPE = 32
MLA_V = 64
MLA_Q_RANK = 384
MLA_KV_RANK = 256

HEAD_DIM = 64
PARTIAL_ROT_DIM = HEAD_DIM // 4
DIL_CONFIGS = ((128, 1), (512, 4), (2048, 16))
DIL_GROUPS = len(DIL_CONFIGS)
DIL_HEADS = 6
DIFF_HEADS = 4
D_FF = 2816
CONV_WIDTH = 3

GATE_COLS = 3 * D_MODEL
MLA_COLS = MLA_Q_RANK + MLA_KV_RANK + MLA_ROPE
DIL_COLS = DIL_GROUPS * 3 * DIL_HEADS * HEAD_DIM
DIFF_COLS = 3 * DIFF_HEADS * 2 * HEAD_DIM
DIL_OUT = DIL_HEADS * HEAD_DIM
DIFF_OUT = DIFF_HEADS * 2 * HEAD_DIM

LANES = 128
LOG2E = 1.4426950408889634
NEG_BIG = -1e30
VMEM_LIMIT = 56 * 1024 * 1024

TM_NORM = 1024
TM_PROJ = 1024
TM_MERGE = 512
MERGE_SPLIT = 2
TM_FFN = 512
TF_FFN = 256
FFN_HALO = 16
TQ_MLA = 2048
TQ_DIFF = 1024
TK_MLA = 1024
TK_DIFF = 512
TJ_DIL = 1024
DIL_RADIUS = 64
DIL_QSUB = 128
DIL_KWIN = DIL_QSUB + 2 * DIL_RADIUS


def _cparams(sem):
    return pltpu.CompilerParams(dimension_semantics=sem, vmem_limit_bytes=VMEM_LIMIT)


def _rms(x, g):
    ms = jnp.mean(x * x, axis=-1, keepdims=True)
    return x * lax.rsqrt(ms + NORM_EPS) * g


def _rope128(z, c, sa, sb, shift):
    return z * c + pltpu.roll(z, LANES - shift, 1) * sa + pltpu.roll(z, shift, 1) * sb


def _resident(a):
    return pl.BlockSpec(a.shape, lambda *_: (0,) * a.ndim, pipeline_mode=pl.Buffered(1))


def _norm_kernel(x_ref, g_ref, o_ref):
    o_ref[...] = _rms(x_ref[...], g_ref[...]).astype(o_ref.dtype)


def _norm_cast(x2, g):
    t, d = x2.shape
    tm = TM_NORM
    return pl.pallas_call(
        _norm_kernel,
        grid=(t // tm,),
        in_specs=[pl.BlockSpec((tm, d), lambda i: (i, 0)), _resident(g)],
        out_specs=pl.BlockSpec((tm, d), lambda i: (i, 0)),
        out_shape=jax.ShapeDtypeStruct((t, d), BF16),
        name="pre_norm",
        compiler_params=_cparams(("parallel",)),
    )(x2, g)


def _inproj_kernel(h_ref, w_ref, *rest, rope_chunks, n_chunks, dil):
    rest = list(rest)
    scr_ref = rest.pop() if dil > 1 else None
    o_ref = rest.pop()
    y = jnp.dot(h_ref[...], w_ref[...], preferred_element_type=F32)
    for c in range(n_chunks):
        yc = y[:, c * LANES:(c + 1) * LANES]
        if c < rope_chunks:
            c_ref, sa_ref, sb_ref = rest
            yc = _rope128(yc, c_ref[...], sa_ref[...], sb_ref[...], PARTIAL_ROT_DIM // 2)
        if dil > 1:
            scr_ref[c] = yc
        else:
            o_ref[:, c * LANES:(c + 1) * LANES] = yc.astype(o_ref.dtype)
    if dil > 1:
        rows = y.shape[0] // dil
        for r in range(dil):
            for c in range(n_chunks):
                col = (r * n_chunks + c) * LANES
                o_ref[:, col:col + LANES] = scr_ref[c, pl.ds(r, rows, stride=dil), :].astype(o_ref.dtype)


def _inproj(h, w, tables, *, tn, rope_chunks, dil=1, tm=TM_PROJ):
    t, d = h.shape
    n = w.shape[1]
    assert dil == 1 or n == tn
    in_specs = [
        pl.BlockSpec((tm, d), lambda j, i: (i, 0)),
        pl.BlockSpec((d, tn), lambda j, i: (0, j)),
    ]
    args = [h, w]
    if rope_chunks:
        in_specs += [pl.BlockSpec((tm, LANES), lambda j, i: (i, 0))] * 3
        args += list(tables)
    if dil > 1:
        out_spec = pl.BlockSpec((tm // dil, dil * tn), lambda j, i: (i, 0))
        out_shape = jax.ShapeDtypeStruct((t // dil, dil * tn), BF16)
        scratch = [pltpu.VMEM((tn // LANES, tm, LANES), F32)]
    else:
        out_spec = pl.BlockSpec((tm, tn), lambda j, i: (i, j))
        out_shape = jax.ShapeDtypeStruct((t, n), BF16)
        scratch = []
    return pl.pallas_call(
        functools.partial(_inproj_kernel, rope_chunks=rope_chunks, n_chunks=tn // LANES, dil=dil),
        grid=(n // tn, t // tm),
        in_specs=in_specs,
        out_specs=out_spec,
        out_shape=out_shape,
        scratch_shapes=scratch,
        name=f"inproj_rope_d{dil}" if rope_chunks else "inproj_gates",
        compiler_params=_cparams(("parallel", "parallel")),
    )(*args)


def _mla_prep_kernel(h_ref, win_ref, gq_ref, gkv_ref, wuq_ref, wuk_ref, wuv_ref,
                     c_ref, sa_ref, sb_ref, q_ref, k_ref, v_ref):
    y = jnp.dot(h_ref[...], win_ref[...], preferred_element_type=F32)
    cq = y[:, :MLA_Q_RANK]
    ckv = y[:, MLA_Q_RANK:MLA_Q_RANK + MLA_KV_RANK]
    kr = y[:, MLA_Q_RANK + MLA_KV_RANK:]
    qn = _rms(cq, gq_ref[...]).astype(BF16)
    kvn = _rms(ckv, gkv_ref[...]).astype(BF16)
    q = jnp.dot(qn, wuq_ref[...], preferred_element_type=F32)
    kn = jnp.dot(kvn, wuk_ref[...], preferred_element_type=F32)
    vv = jnp.dot(kvn, wuv_ref[...], preferred_element_type=F32)
    c, sa, sb = c_ref[...], sa_ref[...], sb_ref[...]
    krr = _rope128(kr, c, sa, sb, MLA_ROPE // 2)
    lane = lax.broadcasted_iota(jnp.int32, krr.shape, 1)
    for hd in range(MLA_HEADS):
        sl = slice(hd * LANES, (hd + 1) * LANES)
        q_ref[:, sl] = _rope128(q[:, sl], c, sa, sb, MLA_ROPE // 2).astype(BF16)
        k_ref[:, sl] = jnp.where(lane < MLA_NOPE, kn[:, sl], krr).astype(BF16)
        v_ref[:, sl] = jnp.where(lane < MLA_V, vv[:, sl], 1.0).astype(BF16)


def _mla_prep(h, win, gq, gkv, wuq, wuk, wuv, tables):
    t, d = h.shape
    tm = TM_PROJ
    hw = MLA_HEADS * LANES
    row = lambda w_: pl.BlockSpec((tm, w_), lambda i: (i, 0))
    out = jax.ShapeDtypeStruct((t, hw), BF16)
    wts = [win, gq, gkv, wuq, wuk, wuv]
    return pl.pallas_call(
        _mla_prep_kernel,
        grid=(t // tm,),
        in_specs=[row(d)] + [_resident(a) for a in wts] + [row(LANES)] * 3,
        out_specs=[row(hw)] * 3,
        out_shape=[out] * 3,
        name="mla_prep",
        compiler_params=_cparams(("parallel",)),
    )(h, *wts, *tables)


def _flash_sweep(q, k_at, v_at, m_ref, acc_ref, *, seq, tk):
    nv = acc_ref.shape[1]
    for c in range(seq // tk):
        sl = slice(c * tk, (c + 1) * tk)
        s = lax.dot_general(q, k_at(sl), (((1,), (1,)), ((), ())), preferred_element_type=F32)
        m_cur = jnp.max(s, axis=1, keepdims=True)
        if c == 0:
            m_new = jnp.broadcast_to(m_cur, m_ref.shape)
        else:
            m_prev = m_ref[...]
            m_new = jnp.maximum(m_prev, m_cur)
            alpha = jnp.exp2(m_prev - m_new)
        p = jnp.exp2((s - jnp.tile(m_new, (1, tk // LANES))).astype(BF16))
        pv = jnp.dot(p, v_at(sl), preferred_element_type=F32)
        acc_ref[...] = pv if c == 0 else jnp.tile(alpha, (1, nv // LANES)) * acc_ref[...] + pv
        m_ref[...] = m_new


def _flash_mla_kernel(q_ref, k_ref, v_ref, o_ref, m_ref, acc_ref, *, tk):
    _flash_sweep(q_ref[0], lambda sl: k_ref[0, sl, :], lambda sl: v_ref[0, sl, :], m_ref, acc_ref,
                 seq=k_ref.shape[1], tk=tk)
    acc = acc_ref[...]
    lane = lax.broadcasted_iota(jnp.int32, acc.shape, 1)
    o_ref[0] = jnp.where(lane < MLA_V, acc / pltpu.roll(acc, MLA_V, 1), 0.0).astype(o_ref.dtype)


def _flash_mla(q, k, v):
    b, s, hw = q.shape
    nh = hw // LANES
    tq, tk = min(TQ_MLA, s), min(TK_MLA, s)
    qspec = pl.BlockSpec((1, tq, LANES), lambda bi, h, i: (bi, i, h))
    kspec = pl.BlockSpec((1, s, LANES), lambda bi, h, i: (bi, 0, h))
    return pl.pallas_call(
        functools.partial(_flash_mla_kernel, tk=tk),
        grid=(b, nh, s // tq),
        in_specs=[qspec, kspec, kspec],
        out_specs=qspec,
        out_shape=jax.ShapeDtypeStruct((b, s, hw), BF16),
        scratch_shapes=[pltpu.VMEM((tq, LANES), F32)] * 2,
        name="flash_mla",
        compiler_params=_cparams(("parallel", "parallel", "arbitrary")),
    )(q, k, v)


def _flash_diff_kernel(q_ref, k_ref, v_ref, lam_ref, subln_ref, o_ref, v1_ref, m_ref, acc_ref, *, tk, lam_init):
    @pl.when(pl.program_id(2) == 0)
    def _():
        v1_ref[:, :LANES] = v_ref[0]
        v1_ref[:, LANES:] = jnp.ones((v1_ref.shape[0], LANES), BF16)

    q = q_ref[0]
    tq = q.shape[0]
    lane = lax.broadcasted_iota(jnp.int32, q.shape, 1)
    zero = jnp.zeros_like(q)
    q2 = jnp.concatenate([jnp.where(lane < HEAD_DIM, q, zero), jnp.where(lane >= HEAD_DIM, q, zero)], axis=0)
    _flash_sweep(q2, lambda sl: k_ref[0, sl, :], lambda sl: v1_ref[sl, :], m_ref, acc_ref,
                 seq=k_ref.shape[1], tk=tk)
    acc = acc_ref[...]
    o12 = acc[:, :LANES] / acc[:, LANES:]
    lv = lam_ref[...]
    lam = (jnp.exp(jnp.sum(lv[0:1] * lv[1:2], axis=1, keepdims=True))
           - jnp.exp(jnp.sum(lv[2:3] * lv[3:4], axis=1, keepdims=True)) + lam_init)
    o = o12[:tq] - lam * o12[tq:]
    o_ref[0] = (_rms(o, subln_ref[...]) * (1.0 - lam_init)).astype(o_ref.dtype)


def _flash_diff(proj, lam_vecs, subln, lam_init):
    b, s, _ = proj.shape
    nh = DIFF_HEADS
    tq, tk = min(TQ_DIFF, s), min(TK_DIFF, s)
    return pl.pallas_call(
        functools.partial(_flash_diff_kernel, tk=tk, lam_init=lam_init),
        grid=(b, nh, s // tq),
        in_specs=[
            pl.BlockSpec((1, tq, LANES), lambda bi, h, i: (bi, i, h)),
            pl.BlockSpec((1, s, LANES), lambda bi, h, i: (bi, 0, nh + h)),
            pl.BlockSpec((1, s, LANES), lambda bi, h, i: (bi, 0, 2 * nh + h)),
            pl.BlockSpec(lam_vecs.shape, lambda bi, h, i: (0, 0)),
            pl.BlockSpec(subln.shape, lambda bi, h, i: (0, 0)),
        ],
        out_specs=pl.BlockSpec((1, tq, LANES), lambda bi, h, i: (bi, i, h)),
        out_shape=jax.ShapeDtypeStruct((b, s, DIFF_OUT), BF16),
        scratch_shapes=[pltpu.VMEM((s, 2 * LANES), BF16), pltpu.VMEM((2 * tq, LANES), F32),
                        pltpu.VMEM((2 * tq, 2 * LANES), F32)],
        name="flash_diff",
        compiler_params=_cparams(("parallel", "parallel", "arbitrary")),
    )(proj, proj, proj, lam_vecs, subln)


def _dil_kernel(q_ref, kp_ref, kc_ref, kn_ref, vp_ref, vc_ref, vn_ref, o_ref, lse_ref, *, tj, lsub):
    j0 = pl.program_id(2) * tj
    r = DIL_RADIUS
    qs = DIL_QSUB
    row = lax.broadcasted_iota(jnp.int32, (2 * qs, DIL_KWIN), 0) & (qs - 1)
    col = lax.broadcasted_iota(jnp.int32, (2 * qs, DIL_KWIN), 1)
    band = jnp.abs(col - r - row) <= r
    biases = []
    for i in range(tj // qs):
        kpos = j0 + (i * qs - r) + col
        biases.append(jnp.where(band & (kpos >= 0) & (kpos < lsub), 0.0, NEG_BIG))
    low = lax.broadcasted_iota(jnp.int32, (qs, LANES), 1) < HEAD_DIM
    ones = jnp.ones((tj + 2 * r, LANES), BF16)
    for c in range(DIL_OUT // LANES):
        sl = slice(c * LANES, (c + 1) * LANES)
        kcat = jnp.concatenate([kp_ref[0, :, sl], kc_ref[0, :, sl], kn_ref[0, :, sl]], axis=0)
        vcat = jnp.concatenate([vp_ref[0, :, sl], vc_ref[0, :, sl], vn_ref[0, :, sl]], axis=0)
        vcat = jnp.concatenate([vcat, ones], axis=1)
        for i in range(tj // qs):
            rows = slice(i * qs, (i + 1) * qs)
            q = q_ref[0, rows, sl]
            zero = jnp.zeros_like(q)
            q2 = jnp.concatenate([jnp.where(low, q, zero), jnp.where(low, zero, q)], axis=0)
            kw = kcat[i * qs:i * qs + DIL_KWIN]
            vw = vcat[i * qs:i * qs + DIL_KWIN]
            s = lax.dot_general(q2, kw, (((1,), (1,)), ((), ())), preferred_element_type=F32) + biases[i]
            m = jnp.max(s, axis=1, keepdims=True)
            p = jnp.exp2((s - m).astype(BF16))
            pv = jnp.dot(p, vw, preferred_element_type=F32)
            den = pv[:, LANES:]
            o2 = pv[:, :LANES] / den
            lse2 = m + jnp.log2(den)
            o_ref[0, rows, sl] = jnp.where(low, o2[:qs], o2[qs:]).astype(o_ref.dtype)
            lse_ref[0, rows, sl] = jnp.where(low, lse2[:qs], lse2[qs:])


def _dilated(proj, dil, b, s):
    w = DIL_OUT
    lsub = s // dil
    tj = min(TJ_DIL, lsub)
    nblk = lsub // tj
    view = proj.reshape(b, lsub, dil * 3 * w)

    hr = DIL_RADIUS
    nhalo = lsub // hr

    def spec(t, shift):
        if shift == 0:
            return pl.BlockSpec((1, tj, w), lambda bi, ri, ji: (bi, ji, ri * 3 + t))
        if shift < 0:
            return pl.BlockSpec((1, hr, w), lambda bi, ri, ji: (bi, jnp.maximum(ji * (tj // hr) - 1, 0), ri * 3 + t))
        return pl.BlockSpec((1, hr, w), lambda bi, ri, ji: (bi, jnp.minimum((ji + 1) * (tj // hr), nhalo - 1), ri * 3 + t))

    ospec = pl.BlockSpec((1, tj, w), lambda bi, ri, ji: (bi, ji, ri))
    o, lse = pl.pallas_call(
        functools.partial(_dil_kernel, tj=tj, lsub=lsub),
        grid=(b, dil, nblk),
        in_specs=[spec(0, 0), spec(1, -1), spec(1, 0), spec(1, 1), spec(2, -1), spec(2, 0), spec(2, 1)],
        out_specs=[ospec, ospec],
        out_shape=[jax.ShapeDtypeStruct((b, lsub, dil * w), BF16), jax.ShapeDtypeStruct((b, lsub, dil * w), F32)],
        name=f"dilated_d{dil}",
        compiler_params=_cparams(("parallel", "parallel", "parallel")),
    )(view, view, view, view, view, view, view)
    return o.reshape(b * lsub, dil * w), lse.reshape(b * lsub, dil * w)


def _merge_kernel(x_ref, gate_ref, omla_ref, od0, od1, od2, ls0, ls1, ls2, odiff_ref,
                  wbm_ref, wbd_ref, wbf_ref, wo_ref, pg_ref, gn_ref, o_ref, hn_ref, scr_ref):
    tm = x_ref.shape[0]
    nch = DIL_OUT // LANES

    def natural(ref, dil):
        if dil == 1:
            return ref[...]
        rows = tm // dil
        for r in range(dil):
            for c in range(nch):
                col = (r * nch + c) * LANES
                scr_ref[c, pl.ds(r, rows, stride=dil), :] = ref[:, col:col + LANES].astype(F32)
        return jnp.concatenate([scr_ref[c] for c in range(nch)], axis=1)

    dils = [dl for _, dl in DIL_CONFIGS]
    l0, l1, l2 = (natural(r_, dl) for r_, dl in zip((ls0, ls1, ls2), dils))
    mx = jnp.maximum(jnp.maximum(l0, l1), l2)
    e0, e1, e2 = jnp.exp2(l0 - mx), jnp.exp2(l1 - mx), jnp.exp2(l2 - mx)
    den = e0 + e1 + e2
    num = e0 * natural(od0, dils[0])
    num = num + e1 * natural(od1, dils[1])
    num = num + e2 * natural(od2, dils[2])
    odil = (num / den).astype(BF16)
    d = D_MODEL
    sub = tm // MERGE_SPLIT
    for part in range(MERGE_SPLIT):
        rs = slice(part * sub, (part + 1) * sub)
        sig = lambda j: jax.nn.sigmoid(gate_ref[rs, j * d:(j + 1) * d].astype(F32))
        y = sig(0) * jnp.dot(omla_ref[rs, :], wbm_ref[...], preferred_element_type=F32)
        y = y + sig(1) * jnp.dot(odil[rs], wbd_ref[...], preferred_element_type=F32)
        y = y + sig(2) * jnp.dot(odiff_ref[rs, :], wbf_ref[...], preferred_element_type=F32)
        z = jnp.dot(y.astype(BF16), wo_ref[...], preferred_element_type=F32)
        xn = x_ref[rs, :] + _rms(z, pg_ref[...])
        o_ref[rs, :] = xn
        hn_ref[rs, :] = _rms(xn, gn_ref[...]).astype(hn_ref.dtype)


def _merge(x2, gates, omla, odil, lses, odiff, wbm, wbd, wbf, wo, pg, g_next):
    t, d = x2.shape
    tm = TM_MERGE
    row = lambda a: pl.BlockSpec((tm * a.shape[0] // t, a.shape[1]), lambda i: (i, 0))
    acts = [x2, gates, omla, *odil, *lses, odiff]
    wts = [wbm, wbd, wbf, wo, pg, g_next]
    return pl.pallas_call(
        _merge_kernel,
        grid=(t // tm,),
        in_specs=[row(a) for a in acts] + [_resident(a) for a in wts],
        out_specs=[pl.BlockSpec((tm, d), lambda i: (i, 0))] * 2,
        out_shape=[jax.ShapeDtypeStruct((t, d), F32), jax.ShapeDtypeStruct((t, d), BF16)],
        scratch_shapes=[pltpu.VMEM((DIL_OUT // LANES, tm, LANES), F32)],
        name="merge",
        compiler_params=_cparams(("parallel",)),
    )(*acts, *wts)


def _ffn_kernel(hp_ref, h_ref, hn_ref, x_ref, wup_ref, cw_ref, wd_ref, pg_ref, *rest, tm, seq, has_next):
    if has_next:
        gn_ref, o_ref, hnext_ref, hext_ref, a_ref = rest
    else:
        o_ref, hext_ref, a_ref = rest
    i = pl.program_id(0)
    hl = FFN_HALO
    keep_p = (i * tm) % seq != 0
    keep_n = ((i + 1) * tm) % seq != 0
    hext_ref[0:hl, :] = jnp.where(keep_p, hp_ref[...], jnp.zeros_like(hp_ref))
    hext_ref[hl:hl + tm, :] = h_ref[...]
    hext_ref[hl + tm:2 * hl + tm, :] = jnp.where(keep_n, hn_ref[...], jnp.zeros_like(hn_ref))
    hx = hext_ref[...]
    n_ext = tm + 2 * hl

    def conv(u, cw):
        up = pltpu.roll(u, 1, 0)
        dn = pltpu.roll(u, n_ext - 1, 0)
        z = cw[0:1] * up + cw[1:2] * u + cw[2:3] * dn
        return z[hl:hl + tm]

    for c in range(D_FF // TF_FFN):
        sg = slice(c * TF_FFN, (c + 1) * TF_FFN)
        sv = slice(D_FF + c * TF_FFN, D_FF + (c + 1) * TF_FFN)
        ug = conv(jnp.dot(hx, wup_ref[:, sg], preferred_element_type=F32), cw_ref[:, sg])
        uv = conv(jnp.dot(hx, wup_ref[:, sv], preferred_element_type=F32), cw_ref[:, sv])
        a_ref[:, sg] = (jax.nn.gelu(ug, approximate=True) * uv).astype(BF16)
    y = jnp.dot(a_ref[...], wd_ref[...], preferred_element_type=F32)
    xn = x_ref[...] + _rms(y, pg_ref[...])
    o_ref[...] = xn
    if has_next:
        hnext_ref[...] = _rms(xn, gn_ref[...]).astype(hnext_ref.dtype)


def _ffn(x2, h, wup, convw, wdown, pg, g_next, seq):
    t, d = x2.shape
    tm = min(TM_FFN, seq)
    hl = FFN_HALO
    rb = tm // hl
    nhb = t // hl
    has_next = g_next is not None
    row = pl.BlockSpec((tm, d), lambda i: (i, 0))
    wts = [wup, convw, wdown, pg] + ([g_next] if has_next else [])
    outs = [jax.ShapeDtypeStruct((t, d), F32)] + ([jax.ShapeDtypeStruct((t, d), BF16)] if has_next else [])
    res = pl.pallas_call(
        functools.partial(_ffn_kernel, tm=tm, seq=seq, has_next=has_next),
        grid=(t // tm,),
        in_specs=[
            pl.BlockSpec((hl, d), lambda i: (jnp.maximum(i * rb - 1, 0), 0)),
            row,
            pl.BlockSpec((hl, d), lambda i: (jnp.minimum((i + 1) * rb, nhb - 1), 0)),
            row,
        ] + [_resident(a) for a in wts],
        out_specs=[row] * len(outs),
        out_shape=outs,
        scratch_shapes=[pltpu.VMEM((tm + 2 * hl, d), BF16), pltpu.VMEM((tm, D_FF), BF16)],
        name="conv_ffn",
        compiler_params=_cparams(("parallel",)),
    )(h, h, h, x2, *wts)
    return (res[0], res[1]) if has_next else (res[0], None)


def _rope_tables(positions, rot_dim, period, base):
    half = rot_dim // 2
    inv = ROPE_THETA ** (-jnp.arange(0, rot_dim, 2, dtype=F32) / rot_dim)
    ang = positions.astype(F32).reshape(-1, 1) * inv[None, :]
    cos, sin = jnp.cos(ang), jnp.sin(ang)
    lane = jnp.arange(LANES)
    rel = lane % period - base
    idx = jnp.clip(rel, 0, rot_dim - 1) % half
    first = (rel >= 0) & (rel < half)
    second = (rel >= half) & (rel < rot_dim)
    cg, sg = cos[:, idx], sin[:, idx]
    c = jnp.where((first | second)[None, :], cg, 1.0)
    sa = jnp.where(first[None, :], -sg, 0.0)
    sb = jnp.where(second[None, :], sg, 0.0)
    return c, sa, sb


def _layer_weights(l, w_in, mla_w_uq, mla_w_ukv, w_branch_mla):
    d = D_MODEL
    wi = w_in[l]
    o1 = GATE_COLS
    o2 = o1 + MLA_COLS
    o3 = o2 + DIL_COLS
    w_gate = wi[:, :o1].astype(BF16)
    wm = wi[:, o1:o2]
    z = lambda n: jnp.zeros((d, n), F32)
    w_mla_in = jnp.concatenate(
        [wm[:, :MLA_Q_RANK + MLA_KV_RANK], z(MLA_NOPE), wm[:, MLA_Q_RANK + MLA_KV_RANK:], z(LANES - MLA_NOPE - MLA_ROPE)],
        axis=1).astype(BF16)
    qscale = LOG2E * HEAD_DIM ** -0.5
    wd = wi[:, o2:o3].reshape(d, DIL_GROUPS, 3, DIL_OUT)
    wd = wd * jnp.array([qscale, 1.0, 1.0], F32)[None, None, :, None]
    w_dil = wd.reshape(d, DIL_COLS).astype(BF16)
    wf = wi[:, o3:].reshape(d, 3, DIFF_OUT)
    wf = wf * jnp.array([qscale, 1.0, 1.0], F32)[None, :, None]
    w_diff = wf.reshape(d, DIFF_COLS).astype(BF16)
    qk = MLA_NOPE + MLA_ROPE
    wq = mla_w_uq[l].reshape(MLA_Q_RANK, MLA_HEADS, qk) * (LOG2E * qk ** -0.5)
    wq = jnp.pad(wq, ((0, 0), (0, 0), (0, LANES - qk))).reshape(MLA_Q_RANK, MLA_HEADS * LANES).astype(BF16)
    wkv = mla_w_ukv[l].reshape(MLA_KV_RANK, MLA_HEADS, MLA_NOPE + MLA_V)
    pad_head = lambda w_: jnp.pad(w_, ((0, 0), (0, 0), (0, LANES - w_.shape[2]))).reshape(
        MLA_KV_RANK, MLA_HEADS * LANES).astype(BF16)
    wk = pad_head(wkv[:, :, :MLA_NOPE])
    wv = pad_head(wkv[:, :, MLA_NOPE:])
    wb = w_branch_mla[l].reshape(MLA_HEADS, MLA_V, d)
    wb = jnp.pad(wb, ((0, 0), (0, LANES - MLA_V), (0, 0))).reshape(MLA_HEADS * LANES, d).astype(BF16)
    return w_gate, w_mla_in, w_dil, w_diff, wq, wk, wv, wb


def _lambda_init(layer):
    return 0.8 - 0.6 * math.exp(-0.3 * layer)


def kernel(x, positions, attn_pre_norm, w_in, mla_q_norm, mla_w_uq, mla_kv_norm, mla_w_ukv, diff_lambda,
           diff_subln, w_branch_mla, w_branch_dil, w_branch_diff, w_out, attn_post_norm, ffn_pre_norm,
           w_up, ffn_conv, w_down, ffn_post_norm):
    b, s, d = x.shape
    t = b * s
    depth = w_in.shape[0]
    assert d == D_MODEL and t % max(TM_NORM, TM_PROJ, TM_MERGE) == 0 and s % TM_FFN == 0
    assert s % min(TQ_MLA, s) == 0 and s % min(TK_MLA, s) == 0 and s % min(TQ_DIFF, s) == 0 and s % min(TK_DIFF, s) == 0
    assert all(s % (dl * DIL_QSUB) == 0 and TM_MERGE % (8 * dl) == 0 for _, dl in DIL_CONFIGS)
    tab_p = _rope_tables(positions, PARTIAL_ROT_DIM, HEAD_DIM, 0)
    tab_m = _rope_tables(positions, MLA_ROPE, LANES, MLA_NOPE)
    x2 = x.reshape(t, d)
    row = lambda a: a.reshape(1, -1)
    h = _norm_cast(x2, row(attn_pre_norm[0]))
    for l in range(depth):
        w_gate, w_mla_in, w_dil, w_diff, wq, wk, wv, wbm = _layer_weights(l, w_in, mla_w_uq, mla_w_ukv, w_branch_mla)
        gates = _inproj(h, w_gate, None, tn=GATE_COLS // 2, rope_chunks=0)
        gw = DIL_COLS // DIL_GROUPS
        p_dil =[_inproj(h, w_dil[:, gi * gw:(gi + 1) * gw], tab_p, tn=gw, rope_chunks=2 * DIL_OUT // LANES, dil=dl)
                 for gi, (_, dl) in enumerate(DIL_CONFIGS)]
        p_diff = _inproj(h, w_diff, tab_p, tn=DIFF_COLS, rope_chunks=2 * DIFF_OUT // LANES)
        q_m, k_m, v_m = _mla_prep(h, w_mla_in, row(mla_q_norm[l]), row(mla_kv_norm[l]), wq, wk, wv, tab_m)
        hw = MLA_HEADS * LANES
        o_mla = _flash_mla(q_m.reshape(b, s, hw), k_m.reshape(b, s, hw), v_m.reshape(b, s, hw)).reshape(t, hw)
        o_diff = _flash_diff(p_diff.reshape(b, s, DIFF_COLS), diff_lambda[l], row(diff_subln[l]),
                             _lambda_init(l)).reshape(t, DIFF_OUT)
        dil = [_dilated(p_dil[gi], dl, b, s) for gi, (_, dl) in enumerate(DIL_CONFIGS)]
        x2, h_ffn = _merge(x2, gates, o_mla, [o for o, _ in dil], [ls for _, ls in dil], o_diff,
                           wbm, w_branch_dil[l].astype(BF16), w_branch_diff[l].astype(BF16), w_out[l].astype(BF16),
                           row(attn_post_norm[l]), row(ffn_pre_norm[l]))
        g_next = row(attn_pre_norm[l + 1]) if l + 1 < depth else None
        x2, h = _ffn(x2, h_ffn, w_up[l].astype(BF16), ffn_conv[l], w_down[l].astype(BF16),
                     row(ffn_post_norm[l]), g_next, s)
    return x2.reshape(b, s, d)
```

```python
import functools
import math

import jax
import jax.numpy as jnp
from jax import lax
from jax.experimental import pallas as pl
from jax.experimental.pallas import tpu as pltpu

F32 = jnp.float32
BF16 = jnp.bfloat16

D_MODEL = 1024
ROPE_THETA = 500000.0
NORM_EPS = 1e-6

MLA_HEADS = 8
MLA_NOPE = 64
MLA_ROPE = 32
MLA_V = 64
MLA_Q_RANK = 384
MLA_KV_RANK = 256

HEAD_DIM = 64
PARTIAL_ROT_DIM = HEAD_DIM // 4
DIL_CONFIGS = ((128, 1), (512, 4), (2048, 16))
DIL_GROUPS = len(DIL_CONFIGS)
DIL_HEADS = 6
DIFF_HEADS = 4
D_FF = 2816
CONV_WIDTH = 3

GATE_COLS = 3 * D_MODEL
MLA_COLS = MLA_Q_RANK + MLA_KV_RANK + MLA_ROPE
DIL_COLS = DIL_GROUPS * 3 * DIL_HEADS * HEAD_DIM
DIFF_COLS = 3 * DIFF_HEADS * 2 * HEAD_DIM
DIL_OUT = DIL_HEADS * HEAD_DIM
DIFF_OUT = DIFF_HEADS * 2 * HEAD_DIM

LANES = 128
LOG2E = 1.4426950408889634
NEG_BIG = -1e30
VMEM_LIMIT = 56 * 1024 * 1024

TM_NORM = 1024
TM_PROJ = 2048
TM_PROJ_WIDE = 1024
DIL_WIDE = 16
TM_MERGE = 512
MERGE_SPLIT = 2
TM_FFN = 512
TF_FFN = 256
FFN_HALO = 16
TQ_MLA = 2048
TQ_DIFF = 1024
TK_MLA = 1024
TK_DIFF = 512
TJ_DIL = 1024
DIL_RADIUS = 64
DIL_QSUB = 128
DIL_KWIN = DIL_QSUB + 2 * DIL_RADIUS


def _cparams(sem):
    return pltpu.CompilerParams(dimension_semantics=sem, vmem_limit_bytes=VMEM_LIMIT)


def _rms(x, g):
    ms = jnp.mean(x * x, axis=-1, keepdims=True)
    return x * lax.rsqrt(ms + NORM_EPS) * g


def _rope128(z, c, sa, sb, shift):
    return z * c + pltpu.roll(z, LANES - shift, 1) * sa + pltpu.roll(z, shift, 1) * sb


def _resident(a):
    return pl.BlockSpec(a.shape, lambda *_: (0,) * a.ndim, pipeline_mode=pl.Buffered(1))


def _norm_kernel(x_ref, g_ref, o_ref):
    o_ref[...] = _rms(x_ref[...], g_ref[...]).astype(o_ref.dtype)


def _norm_cast(x2, g):
    t, d = x2.shape
    tm = TM_NORM
    return pl.pallas_call(
        _norm_kernel,
        grid=(t // tm,),
        in_specs=[pl.BlockSpec((tm, d), lambda i: (i, 0)), _resident(g)],
        out_specs=pl.BlockSpec((tm, d), lambda i: (i, 0)),
        out_shape=jax.ShapeDtypeStruct((t, d), BF16),
        name="pre_norm",
        compiler_params=_cparams(("parallel",)),
    )(x2, g)


def _inproj_kernel(h_ref, w_ref, *rest, rope_chunks, n_chunks, dil):
    rest = list(rest)
    scr_ref = rest.pop() if dil > 1 else None
    o_ref = rest.pop()
    y = jnp.dot(h_ref[...], w_ref[...], preferred_element_type=F32)
    for c in range(n_chunks):
        yc = y[:, c * LANES:(c + 1) * LANES]
        if c < rope_chunks:
            c_ref, sa_ref, sb_ref = rest
            yc = _rope128(yc, c_ref[...], sa_ref[...], sb_ref[...], PARTIAL_ROT_DIM // 2)
        if dil > 1:
            scr_ref[c] = yc
        else:
            o_ref[:, c * LANES:(c + 1) * LANES] = yc.astype(o_ref.dtype)
    if dil > 1:
        rows = y.shape[0] // dil
        for r in range(dil):
            for c in range(n_chunks):
                col = (r * n_chunks + c) * LANES
                o_ref[:, col:col + LANES] = scr_ref[c, pl.ds(r, rows, stride=dil), :].astype(o_ref.dtype)


def _inproj(h, w, tables, *, tn, rope_chunks, dil=1, tm=TM_PROJ):
    t, d = h.shape
    n = w.shape[1]
    assert dil == 1 or n == tn
    in_specs = [
        pl.BlockSpec((tm, d), lambda j, i: (i, 0)),
        pl.BlockSpec((d, tn), lambda j, i: (0, j)),
    ]
    args = [h, w]
    if rope_chunks:
        in_specs += [pl.BlockSpec((tm, LANES), lambda j, i: (i, 0))] * 3
        args += list(tables)
    if dil > 1:
        out_spec = pl.BlockSpec((tm // dil, dil * tn), lambda j, i: (i, 0))
        out_shape = jax.ShapeDtypeStruct((t // dil, dil * tn), BF16)
        scratch = [pltpu.VMEM((tn // LANES, tm, LANES), F32)]
    else:
        out_spec = pl.BlockSpec((tm, tn), lambda j, i: (i, j))
        out_shape = jax.ShapeDtypeStruct((t, n), BF16)
        scratch = []
    return pl.pallas_call(
        functools.partial(_inproj_kernel, rope_chunks=rope_chunks, n_chunks=tn // LANES, dil=dil),
        grid=(n // tn, t // tm),
        in_specs=in_specs,
        out_specs=out_spec,
        out_shape=out_shape,
        scratch_shapes=scratch,
        name=f"inproj_rope_d{dil}" if rope_chunks else "inproj_gates",
        compiler_params=_cparams(("parallel", "parallel")),
    )(*args)


def _mla_prep_kernel(h_ref, win_ref, gq_ref, gkv_ref, wuq_ref, wuk_ref, wuv_ref,
                     c_ref, sa_ref, sb_ref, q_ref, k_ref, v_ref):
    y = jnp.dot(h_ref[...], win_ref[...], preferred_element_type=F32)
    cq = y[:, :MLA_Q_RANK]
    ckv = y[:, MLA_Q_RANK:MLA_Q_RANK + MLA_KV_RANK]
    kr = y[:, MLA_Q_RANK + MLA_KV_RANK:]
    qn = _rms(cq, gq_ref[...]).astype(BF16)
    kvn = _rms(ckv, gkv_ref[...]).astype(BF16)
    q = jnp.dot(qn, wuq_ref[...], preferred_element_type=F32)
    kn = jnp.dot(kvn, wuk_ref[...], preferred_element_type=F32)
    vv = jnp.dot(kvn, wuv_ref[...], preferred_element_type=F32)
    c, sa, sb = c_ref[...], sa_ref[...], sb_ref[...]
    krr = _rope128(kr, c, sa, sb, MLA_ROPE // 2)
    lane = lax.broadcasted_iota(jnp.int32, krr.shape, 1)
    for hd in range(MLA_HEADS):
        sl = slice(hd * LANES, (hd + 1) * LANES)
        q_ref[:, sl] = _rope128(q[:, sl], c, sa, sb, MLA_ROPE // 2).astype(BF16)
        k_ref[:, sl] = jnp.where(lane < MLA_NOPE, kn[:, sl], krr).astype(BF16)
        v_ref[:, sl] = jnp.where(lane < MLA_V, vv[:, sl], 1.0).astype(BF16)


def _mla_prep(h, win, gq, gkv, wuq, wuk, wuv, tables):
    t, d = h.shape
    tm = TM_PROJ
    hw = MLA_HEADS * LANES
    row = lambda w_: pl.BlockSpec((tm, w_), lambda i: (i, 0))
    out = jax.ShapeDtypeStruct((t, hw), BF16)
    wts = [win, gq, gkv, wuq, wuk, wuv]
    return pl.pallas_call(
        _mla_prep_kernel,
        grid=(t // tm,),
        in_specs=[row(d)] + [_resident(a) for a in wts] + [row(LANES)] * 3,
        out_specs=[row(hw)] * 3,
        out_shape=[out] * 3,
        name="mla_prep",
        compiler_params=_cparams(("parallel",)),
    )(h, *wts, *tables)


def _flash_sweep(q, k_at, v_at, m_ref, acc_ref, *, seq, tk):
    nv = acc_ref.shape[1]
    for c in range(seq // tk):
        sl = slice(c * tk, (c + 1) * tk)
        s = lax.dot_general(q, k_at(sl), (((1,), (1,)), ((), ())), preferred_element_type=F32)
        m_cur = jnp.max(s, axis=1, keepdims=True)
        if c == 0:
            m_new = jnp.broadcast_to(m_cur, m_ref.shape)
        else:
            m_prev = m_ref[...]
            m_new = jnp.maximum(m_prev, m_cur)
            alpha = jnp.exp2(m_prev - m_new)
        p = jnp.exp2((s - jnp.tile(m_new, (1, tk // LANES))).astype(BF16))
        pv = jnp.dot(p, v_at(sl), preferred_element_type=F32)
        acc_ref[...] = pv if c == 0 else jnp.tile(alpha, (1, nv // LANES)) * acc_ref[...] + pv
        m_ref[...] = m_new


def _flash_mla_kernel(q_ref, k_ref, v_ref, o_ref, m_ref, acc_ref, *, tk):
    _flash_sweep(q_ref[0], lambda sl: k_ref[0, sl, :], lambda sl: v_ref[0, sl, :], m_ref, acc_ref,
                 seq=k_ref.shape[1], tk=tk)
    acc = acc_ref[...]
    lane = lax.broadcasted_iota(jnp.int32, acc.shape, 1)
    o_ref[0] = jnp.where(lane < MLA_V, acc / pltpu.roll(acc, MLA_V, 1), 0.0).astype(o_ref.dtype)


def _flash_mla(q, k, v):
    b, s, hw = q.shape
    nh = hw // LANES
    tq, tk = min(TQ_MLA, s), min(TK_MLA, s)
    qspec = pl.BlockSpec((1, tq, LANES), lambda bi, h, i: (bi, i, h))
    kspec = pl.BlockSpec((1, s, LANES), lambda bi, h, i: (bi, 0, h))
    return pl.pallas_call(
        functools.partial(_flash_mla_kernel, tk=tk),
        grid=(b, nh, s // tq),
        in_specs=[qspec, kspec, kspec],
        out_specs=qspec,
        out_shape=jax.ShapeDtypeStruct((b, s, hw), BF16),
        scratch_shapes=[pltpu.VMEM((tq, LANES), F32)] * 2,
        name="flash_mla",
        compiler_params=_cparams(("parallel", "parallel", "arbitrary")),
    )(q, k, v)


def _flash_diff_kernel(q_ref, k_ref, v_ref, lam_ref, subln_ref, o_ref, v1_ref, m_ref, acc_ref, *, tk, lam_init):
    @pl.when(pl.program_id(2) == 0)
    def _():
        v1_ref[:, :LANES] = v_ref[0]
        v1_ref[:, LANES:] = jnp.ones((v1_ref.shape[0], LANES), BF16)

    q = q_ref[0]
    tq = q.shape[0]
    lane = lax.broadcasted_iota(jnp.int32, q.shape, 1)
    zero = jnp.zeros_like(q)
    q2 = jnp.concatenate([jnp.where(lane < HEAD_DIM, q, zero), jnp.where(lane >= HEAD_DIM, q, zero)], axis=0)
    _flash_sweep(q2, lambda sl: k_ref[0, sl, :], lambda sl: v1_ref[sl, :], m_ref, acc_ref,
                 seq=k_ref.shape[1], tk=tk)
    acc = acc_ref[...]
    o12 = acc[:, :LANES] / acc[:, LANES:]
    lv = lam_ref[...]
    lam = (jnp.exp(jnp.sum(lv[0:1] * lv[1:2], axis=1, keepdims=True))
           - jnp.exp(jnp.sum(lv[2:3] * lv[3:4], axis=1, keepdims=True)) + lam_init)
    o = o12[:tq] - lam * o12[tq:]
    o_ref[0] = (_rms(o, subln_ref[...]) * (1.0 - lam_init)).astype(o_ref.dtype)


def _flash_diff(proj, lam_vecs, subln, lam_init):
    b, s, _ = proj.shape
    nh = DIFF_HEADS
    tq, tk = min(TQ_DIFF, s), min(TK_DIFF, s)
    return pl.pallas_call(
        functools.partial(_flash_diff_kernel, tk=tk, lam_init=lam_init),
        grid=(b, nh, s // tq),
        in_specs=[
            pl.BlockSpec((1, tq, LANES), lambda bi, h, i: (bi, i, h)),
            pl.BlockSpec((1, s, LANES), lambda bi, h, i: (bi, 0, nh + h)),
            pl.BlockSpec((1, s, LANES), lambda bi, h, i: (bi, 0, 2 * nh + h)),
            pl.BlockSpec(lam_vecs.shape, lambda bi, h, i: (0, 0)),
            pl.BlockSpec(subln.shape, lambda bi, h, i: (0, 0)),
        ],
        out_specs=pl.BlockSpec((1, tq, LANES), lambda bi, h, i: (bi, i, h)),
        out_shape=jax.ShapeDtypeStruct((b, s, DIFF_OUT), BF16),
        scratch_shapes=[pltpu.VMEM((s, 2 * LANES), BF16), pltpu.VMEM((2 * tq, LANES), F32),
                        pltpu.VMEM((2 * tq, 2 * LANES), F32)],
        name="flash_diff",
        compiler_params=_cparams(("parallel", "parallel", "arbitrary")),
    )(proj, proj, proj, lam_vecs, subln)


def _dil_kernel(q_ref, kp_ref, kc_ref, kn_ref, vp_ref, vc_ref, vn_ref, o_ref, lse_ref, *, tj, lsub):
    j0 = pl.program_id(2) * tj
    r = DIL_RADIUS
    qs = DIL_QSUB
    row = lax.broadcasted_iota(jnp.int32, (2 * qs, DIL_KWIN), 0) & (qs - 1)
    col = lax.broadcasted_iota(jnp.int32, (2 * qs, DIL_KWIN), 1)
    band = jnp.abs(col - r - row) <= r
    biases = []
    for i in range(tj // qs):
        kpos = j0 + (i * qs - r) + col
        biases.append(jnp.where(band & (kpos >= 0) & (kpos < lsub), 0.0, NEG_BIG))
    low = lax.broadcasted_iota(jnp.int32, (qs, LANES), 1) < HEAD_DIM
    ones = jnp.ones((tj + 2 * r, LANES), BF16)
    for c in range(DIL_OUT // LANES):
        sl = slice(c * LANES, (c + 1) * LANES)
        kcat = jnp.concatenate([kp_ref[0, :, sl], kc_ref[0, :, sl], kn_ref[0, :, sl]], axis=0)
        vcat = jnp.concatenate([vp_ref[0, :, sl], vc_ref[0, :, sl], vn_ref[0, :, sl]], axis=0)
        vcat = jnp.concatenate([vcat, ones], axis=1)
        for i in range(tj // qs):
            rows = slice(i * qs, (i + 1) * qs)
            q = q_ref[0, rows, sl]
            zero = jnp.zeros_like(q)
            q2 = jnp.concatenate([jnp.where(low, q, zero), jnp.where(low, zero, q)], axis=0)
            kw = kcat[i * qs:i * qs + DIL_KWIN]
            vw = vcat[i * qs:i * qs + DIL_KWIN]
            s = lax.dot_general(q2, kw, (((1,), (1,)), ((), ())), preferred_element_type=F32) + biases[i]
            m = jnp.max(s, axis=1, keepdims=True)
            p = jnp.exp2((s - m).astype(BF16))
            pv = jnp.dot(p, vw, preferred_element_type=F32)
            den = pv[:, LANES:]
            o2 = pv[:, :LANES] / den
            lse2 = m + jnp.log2(den)
            o_ref[0, rows, sl] = jnp.where(low, o2[:qs], o2[qs:]).astype(o_ref.dtype)
            lse_ref[0, rows, sl] = jnp.where(low, lse2[:qs], lse2[qs:])


def _dilated(proj, dil, b, s):
    w = DIL_OUT
    lsub = s // dil
    tj = min(TJ_DIL, lsub)
    nblk = lsub // tj
    view = proj.reshape(b, lsub, dil * 3 * w)

    hr = DIL_RADIUS
    nhalo = lsub // hr

    def spec(t, shift):
        if shift == 0:
            return pl.BlockSpec((1, tj, w), lambda bi, ri, ji: (bi, ji, ri * 3 + t))
        if shift < 0:
            return pl.BlockSpec((1, hr, w), lambda bi, ri, ji: (bi, jnp.maximum(ji * (tj // hr) - 1, 0), ri * 3 + t))
        return pl.BlockSpec((1, hr, w), lambda bi, ri, ji: (bi, jnp.minimum((ji + 1) * (tj // hr), nhalo - 1), ri * 3 + t))

    ospec = pl.BlockSpec((1, tj, w), lambda bi, ri, ji: (bi, ji, ri))
    o, lse = pl.pallas_call(
        functools.partial(_dil_kernel, tj=tj, lsub=lsub),
        grid=(b, dil, nblk),
        in_specs=[spec(0, 0), spec(1, -1), spec(1, 0), spec(1, 1), spec(2, -1), spec(2, 0), spec(2, 1)],
        out_specs=[ospec, ospec],
        out_shape=[jax.ShapeDtypeStruct((b, lsub, dil * w), BF16), jax.ShapeDtypeStruct((b, lsub, dil * w), F32)],
        name=f"dilated_d{dil}",
        compiler_params=_cparams(("parallel", "parallel", "parallel")),
    )(view, view, view, view, view, view, view)
    return o.reshape(b * lsub, dil * w), lse.reshape(b * lsub, dil * w)


def _merge_kernel(x_ref, gate_ref, omla_ref, od0, od1, od2, ls0, ls1, ls2, odiff_ref,
                  wbm_ref, wbd_ref, wbf_ref, wo_ref, pg_ref, gn_ref, o_ref, hn_ref, scr_ref):
    tm = x_ref.shape[0]
    nch = DIL_OUT // LANES

    def natural(ref, dil):
        if dil == 1:
            return ref[...]
        rows = tm // dil
        for r in range(dil):
            for c in range(nch):
                col = (r * nch + c) * LANES
                scr_ref[c, pl.ds(r, rows, stride=dil), :] = ref[:, col:col + LANES].astype(F32)
        return jnp.concatenate([scr_ref[c] for c in range(nch)], axis=1)

    dils = [dl for _, dl in DIL_CONFIGS]
    l0, l1, l2 = (natural(r_, dl) for r_, dl in zip((ls0, ls1, ls2), dils))
    mx = jnp.maximum(jnp.maximum(l0, l1), l2)
    e0, e1, e2 = jnp.exp2(l0 - mx), jnp.exp2(l1 - mx), jnp.exp2(l2 - mx)
    den = e0 + e1 + e2
    num = e0 * natural(od0, dils[0])
    num = num + e1 * natural(od1, dils[1])
    num = num + e2 * natural(od2, dils[2])
    odil = (num / den).astype(BF16)
    d = D_MODEL
    sub = tm // MERGE_SPLIT
    for part in range(MERGE_SPLIT):
        rs = slice(part * sub, (part + 1) * sub)
        sig = lambda j: jax.nn.sigmoid(gate_ref[rs, j * d:(j + 1) * d].astype(F32))
        y = sig(0) * jnp.dot(omla_ref[rs, :], wbm_ref[...], preferred_element_type=F32)
        y = y + sig(1) * jnp.dot(odil[rs], wbd_ref[...], preferred_element_type=F32)
        y = y + sig(2) * jnp.dot(odiff_ref[rs, :], wbf_ref[...], preferred_element_type=F32)
        z = jnp.dot(y.astype(BF16), wo_ref[...], preferred_element_type=F32)
        xn = x_ref[rs, :] + _rms(z, pg_ref[...])
        o_ref[rs, :] = xn
        hn_ref[rs, :] = _rms(xn, gn_ref[...]).astype(hn_ref.dtype)


def _merge(x2, gates, omla, odil, lses, odiff, wbm, wbd, wbf, wo, pg, g_next):
    t, d = x2.shape
    tm = TM_MERGE
    row = lambda a: pl.BlockSpec((tm * a.shape[0] // t, a.shape[1]), lambda i: (i, 0))
    acts = [x2, gates, omla, *odil, *lses, odiff]
    wts = [wbm, wbd, wbf, wo, pg, g_next]
    return pl.pallas_call(
        _merge_kernel,
        grid=(t // tm,),
        in_specs=[row(a) for a in acts] + [_resident(a) for a in wts],
        out_specs=[pl.BlockSpec((tm, d), lambda i: (i, 0))] * 2,
        out_shape=[jax.ShapeDtypeStruct((t, d), F32), jax.ShapeDtypeStruct((t, d), BF16)],
        scratch_shapes=[pltpu.VMEM((DIL_OUT // LANES, tm, LANES), F32)],
        name="merge",
        compiler_params=_cparams(("parallel",)),
    )(*acts, *wts)


def _ffn_kernel(hp_ref, h_ref, hn_ref, x_ref, wup_ref, cw_ref, wd_ref, pg_ref, *rest, tm, seq, has_next):
    if has_next:
        gn_ref, o_ref, hnext_ref, hext_ref, a_ref = rest
    else:
        o_ref, hext_ref, a_ref = rest
    i = pl.program_id(0)
    hl = FFN_HALO
    keep_p = (i * tm) % seq != 0
    keep_n = ((i + 1) * tm) % seq != 0
    hext_ref[0:hl, :] = jnp.where(keep_p, hp_ref[...], jnp.zeros_like(hp_ref))
    hext_ref[hl:hl + tm, :] = h_ref[...]
    hext_ref[hl + tm:2 * hl + tm, :] = jnp.where(keep_n, hn_ref[...], jnp.zeros_like(hn_ref))
    hx = hext_ref[...]
    n_ext = tm + 2 * hl

    def conv(u, cw):
        up = pltpu.roll(u, 1, 0)
        dn = pltpu.roll(u, n_ext - 1, 0)
        z = cw[0:1] * up + cw[1:2] * u + cw[2:3] * dn
        return z[hl:hl + tm]

    for c in range(D_FF // TF_FFN):
        sg = slice(c * TF_FFN, (c + 1) * TF_FFN)
        sv = slice(D_FF + c * TF_FFN, D_FF + (c + 1) * TF_FFN)
        ug = conv(jnp.dot(hx, wup_ref[:, sg], preferred_element_type=F32), cw_ref[:, sg])
        uv = conv(jnp.dot(hx, wup_ref[:, sv], preferred_element_type=F32), cw_ref[:, sv])
        a_ref[:, sg] = (jax.nn.gelu(ug, approximate=True) * uv).astype(BF16)
    y = jnp.dot(a_ref[...], wd_ref[...], preferred_element_type=F32)
    xn = x_ref[...] + _rms(y, pg_ref[...])
    o_ref[...] = xn
    if has_next:
        hnext_ref[...] = _rms(xn, gn_ref[...]).astype(hnext_ref.dtype)


def _ffn(x2, h, wup, convw, wdown, pg, g_next, seq):
    t, d = x2.shape
    tm = min(TM_FFN, seq)
    hl = FFN_HALO
    rb = tm // hl
    nhb = t // hl
    has_next = g_next is not None
    row = pl.BlockSpec((tm, d), lambda i: (i, 0))
    wts = [wup, convw, wdown, pg] + ([g_next] if has_next else [])
    outs = [jax.ShapeDtypeStruct((t, d), F32)] + ([jax.ShapeDtypeStruct((t, d), BF16)] if has_next else [])
    res = pl.pallas_call(
        functools.partial(_ffn_kernel, tm=tm, seq=seq, has_next=has_next),
        grid=(t // tm,),
        in_specs=[
            pl.BlockSpec((hl, d), lambda i: (jnp.maximum(i * rb - 1, 0), 0)),
            row,
            pl.BlockSpec((hl, d), lambda i: (jnp.minimum((i + 1) * rb, nhb - 1), 0)),
            row,
        ] + [_resident(a) for a in wts],
        out_specs=[row] * len(outs),
        out_shape=outs,
        scratch_shapes=[pltpu.VMEM((tm + 2 * hl, d), BF16), pltpu.VMEM((tm, D_FF), BF16)],
        name="conv_ffn",
        compiler_params=_cparams(("parallel",)),
    )(h, h, h, x2, *wts)
    return (res[0], res[1]) if has_next else (res[0], None)


def _rope_tables(positions, rot_dim, period, base):
    half = rot_dim // 2
    inv = ROPE_THETA ** (-jnp.arange(0, rot_dim, 2, dtype=F32) / rot_dim)
    ang = positions.astype(F32).reshape(-1, 1) * inv[None, :]
    cos, sin = jnp.cos(ang), jnp.sin(ang)
    lane = jnp.arange(LANES)
    rel = lane % period - base
    idx = jnp.clip(rel, 0, rot_dim - 1) % half
    first = (rel >= 0) & (rel < half)
    second = (rel >= half) & (rel < rot_dim)
    cg, sg = cos[:, idx], sin[:, idx]
    c = jnp.where((first | second)[None, :], cg, 1.0)
    sa = jnp.where(first[None, :], -sg, 0.0)
    sb = jnp.where(second[None, :], sg, 0.0)
    return c, sa, sb


def _layer_weights(l, w_in, mla_w_uq, mla_w_ukv, w_branch_mla):
    d = D_MODEL
    wi = w_in[l]
    o1 = GATE_COLS
    o2 = o1 + MLA_COLS
    o3 = o2 + DIL_COLS
    w_gate = wi[:, :o1].astype(BF16)
    wm = wi[:, o1:o2]
    z = lambda n: jnp.zeros((d, n), F32)
    w_mla_in = jnp.concatenate(
        [wm[:, :MLA_Q_RANK + MLA_KV_RANK], z(MLA_NOPE), wm[:, MLA_Q_RANK + MLA_KV_RANK:], z(LANES - MLA_NOPE - MLA_ROPE)],
        axis=1).astype(BF16)
    qscale = LOG2E * HEAD_DIM ** -0.5
    wd = wi[:, o2:o3].reshape(d, DIL_GROUPS, 3, DIL_OUT)
    wd = wd * jnp.array([qscale, 1.0, 1.0], F32)[None, None, :, None]
    w_dil = wd.reshape(d, DIL_COLS).astype(BF16)
    wf = wi[:, o3:].reshape(d, 3, DIFF_OUT)
    wf = wf * jnp.array([qscale, 1.0, 1.0], F32)[None, :, None]
    w_diff = wf.reshape(d, DIFF_COLS).astype(BF16)
    qk = MLA_NOPE + MLA_ROPE
    wq = mla_w_uq[l].reshape(MLA_Q_RANK, MLA_HEADS, qk) * (LOG2E * qk ** -0.5)
    wq = jnp.pad(wq, ((0, 0), (0, 0), (0, LANES - qk))).reshape(MLA_Q_RANK, MLA_HEADS * LANES).astype(BF16)
    wkv = mla_w_ukv[l].reshape(MLA_KV_RANK, MLA_HEADS, MLA_NOPE + MLA_V)
    pad_head = lambda w_: jnp.pad(w_, ((0, 0), (0, 0), (0, LANES - w_.shape[2]))).reshape(
        MLA_KV_RANK, MLA_HEADS * LANES).astype(BF16)
    wk = pad_head(wkv[:, :, :MLA_NOPE])
    wv = pad_head(wkv[:, :, MLA_NOPE:])
    wb = w_branch_mla[l].reshape(MLA_HEADS, MLA_V, d)
    wb = jnp.pad(wb, ((0, 0), (0, LANES - MLA_V), (0, 0))).reshape(MLA_HEADS * LANES, d).astype(BF16)
    return w_gate, w_mla_in, w_dil, w_diff, wq, wk, wv, wb


def _lambda_init(layer):
    return 0.8 - 0.6 * math.exp(-0.3 * layer)


def kernel(x, positions, attn_pre_norm, w_in, mla_q_norm, mla_w_uq, mla_kv_norm, mla_w_ukv, diff_lambda,
           diff_subln, w_branch_mla, w_branch_dil, w_branch_diff, w_out, attn_post_norm, ffn_pre_norm,
           w_up, ffn_conv, w_down, ffn_post_norm):
    b, s, d = x.shape
    t = b * s
    depth = w_in.shape[0]
    assert d == D_MODEL and t % max(TM_NORM, TM_PROJ, TM_MERGE) == 0 and s % TM_FFN == 0
    assert s % min(TQ_MLA, s) == 0 and s % min(TK_MLA, s) == 0 and s % min(TQ_DIFF, s) == 0 and s % min(TK_DIFF, s) == 0
    assert all(s % (dl * DIL_QSUB) == 0 and TM_MERGE % (8 * dl) == 0 for _, dl in DIL_CONFIGS)
    tab_p = _rope_tables(positions, PARTIAL_ROT_DIM, HEAD_DIM, 0)
    tab_m = _rope_tables(positions, MLA_ROPE, LANES, MLA_NOPE)
    x2 = x.reshape(t, d)
    row = lambda a: a.reshape(1, -1)
    h = _norm_cast(x2, row(attn_pre_norm[0]))
    for l in range(depth):
        w_gate, w_mla_in, w_dil, w_diff, wq, wk, wv, wbm = _layer_weights(l, w_in, mla_w_uq, mla_w_ukv, w_branch_mla)
        gates = _inproj(h, w_gate, None, tn=GATE_COLS // 2, rope_chunks=0)
        gw = DIL_COLS // DIL_GROUPS
        p_dil = [_inproj(h, w_dil[:, gi * gw:(gi + 1) * gw], tab_p, tn=gw, rope_chunks=2 * DIL_OUT // LANES, dil=dl,
                         tm=TM_PROJ if dl < DIL_WIDE else TM_PROJ_WIDE)
                 for gi, (_, dl) in enumerate(DIL_CONFIGS)]
        p_diff = _inproj(h, w_diff, tab_p, tn=DIFF_COLS, rope_chunks=2 * DIFF_OUT // LANES)
        q_m, k_m, v_m = _mla_prep(h, w_mla_in, row(mla_q_norm[l]), row(mla_kv_norm[l]), wq, wk, wv, tab_m)
        hw = MLA_HEADS * LANES
        o_mla = _flash_mla(q_m.reshape(b, s, hw), k_m.reshape(b, s, hw), v_m.reshape(b, s, hw)).reshape(t, hw)
        o_diff = _flash_diff(p_diff.reshape(b, s, DIFF_COLS), diff_lambda[l], row(diff_subln[l]),
                             _lambda_init(l)).reshape(t, DIFF_OUT)
        dil = [_dilated(p_dil[gi], dl, b, s) for gi, (_, dl) in enumerate(DIL_CONFIGS)]
        x2, h_ffn = _merge(x2, gates, o_mla, [o for o, _ in dil], [ls for _, ls in dil], o_diff,
                           wbm, w_branch_dil[l].astype(BF16), w_branch_diff[l].astype(BF16), w_out[l].astype(BF16),
                           row(attn_post_norm[l]), row(ffn_pre_norm[l]))
        g_next = row(attn_pre_norm[l + 1]) if l + 1 < depth else None
        x2, h = _ffn(x2, h_ffn, w_up[l].astype(BF16), ffn_conv[l], w_down[l].astype(BF16),
                     row(ffn_post_norm[l]), g_next, s)
    return x2.reshape(b, s, d)
```
